```python
import math
import jax, jax.numpy as jnp
from jax import lax
import numpy as np

D_MODEL = 2048
BATCH = 4
SEQ = 4096
DEPTH = 2

HEAD_DIM = 128
SB_HEADS = D_MODEL // 256
DIFF_HEADS = D_MODEL // 512
DIFF_V_DIM = 2 * HEAD_DIM
SB_WIDTH = SB_HEADS * HEAD_DIM
DIFF_QK_WIDTH = DIFF_HEADS * 2 * HEAD_DIM
DIFF_V_WIDTH = DIFF_HEADS * DIFF_V_DIM
EVEN_IN_WIDTH = 3 * SB_WIDTH + 2 * DIFF_QK_WIDTH + DIFF_V_WIDTH
EVEN_MIX_WIDTH = SB_WIDTH + DIFF_V_WIDTH

MLA_HEADS = D_MODEL // 128
MLA_Q_RANK = 512
MLA_KV_RANK = 512
MLA_NOPE_DIM = 128
MLA_ROPE_DIM = 64
MLA_V_DIM = 128
MLA_QK_DIM = MLA_NOPE_DIM + MLA_ROPE_DIM
MLA_IN_WIDTH = MLA_Q_RANK + MLA_KV_RANK + MLA_ROPE_DIM

FFN_DIM = -(-8 * D_MODEL // (3 * 256)) * 256
ROPE_THETA = 10000.0
Q_BLOCK = 128
LN_EPS = 1e-5
RMS_EPS = 1e-6
DN_ALPHA = (2 * DEPTH) ** 0.25
DN_BETA = (8 * DEPTH) ** -0.25
N_EVEN = (DEPTH + 1) // 2
N_ODD = DEPTH // 2

kernel_name = "hybrid_stickbreak_diff_mla_deepnorm"


def layer_norm(x, g, b):
    xf = x.astype(jnp.float32)
    mu = jnp.mean(xf, axis=-1, keepdims=True)
    var = jnp.mean(jnp.square(xf - mu), axis=-1, keepdims=True)
    y = (xf - mu) * lax.rsqrt(var + LN_EPS) * g.astype(jnp.float32) + b.astype(jnp.float32)
    return y.astype(x.dtype)


def rms_norm(x, g, eps):
    xf = x.astype(jnp.float32)
    y = xf * lax.rsqrt(jnp.mean(jnp.square(xf), axis=-1, keepdims=True) + eps)
    return (y * g.astype(jnp.float32)).astype(x.dtype)


def rope_tables(positions, dim):
    inv_freq = ROPE_THETA ** (-jnp.arange(0, dim, 2, dtype=jnp.float32) / dim)
    ang = positions.astype(jnp.float32)[..., None] * inv_freq
    return jnp.cos(ang)[:, :, None, :], jnp.sin(ang)[:, :, None, :]


def apply_rope(x, cos, sin):
    x1, x2 = jnp.split(x.astype(jnp.float32), 2, axis=-1)
    return jnp.concatenate([x1 * cos - x2 * sin, x2 * cos + x1 * sin], axis=-1).astype(x.dtype)


def to_blocks(q):
    *lead, s, d = q.shape
    return jnp.moveaxis(q.reshape(*lead, s // Q_BLOCK, Q_BLOCK, d), -3, 0)


def from_blocks(o):
    o = jnp.moveaxis(o, 0, -3)
    *lead, nb, qb, d = o.shape
    return o.reshape(*lead, nb * qb, d)


def block_positions(i, seq_len):
    q_pos = i * Q_BLOCK + jnp.arange(Q_BLOCK)
    k_pos = jnp.arange(seq_len)
    return q_pos[:, None], k_pos[None, :]


def stick_breaking_attention(q, k, v):
    seq_len, d = q.shape[-2], q.shape[-1]
    scale = d ** -0.5

    def one_block(args):
        i, q_i = args
        qp, kp = block_positions(i, seq_len)
        strict = kp < qp
        z = jnp.einsum('bhqd,bhkd->bhqk', q_i, k).astype(jnp.float32) * scale
        log_beta = jax.nn.log_sigmoid(z)
        log_one_minus = jnp.where(strict, jax.nn.log_sigmoid(-z), 0.0)
        later = lax.cumsum(log_one_minus, axis=3, reverse=True) - log_one_minus
        w = jnp.where(strict, jnp.exp(log_beta + later), 0.0)
        return jnp.einsum('bhqk,bhkd->bhqd', w.astype(v.dtype), v)

    nb = seq_len // Q_BLOCK
    out = lax.map(one_block, (jnp.arange(nb), to_blocks(q)))
    return from_blocks(out)


def differential_attention(q, k, v, lam):
    seq_len, d = q.shape[-2], q.shape[-1]
    scale = d ** -0.5

    def one_block(args):
        i, q_i = args
        qp, kp = block_positions(i, seq_len)
        causal = kp <= qp
        s = jnp.einsum('bhmqd,bhmkd->bhmqk', q_i, k).astype(jnp.float32) * scale
        p = jax.nn.softmax(jnp.where(causal, s, -jnp.inf), axis=-1)
        a = p[:, :, 0] - lam * p[:, :, 1]
        return jnp.einsum('bhqk,bhkd->bhqd', a.astype(v.dtype), v)

    nb = seq_len // Q_BLOCK
    q_b = jnp.moveaxis(to_blocks(q), 0, 0)
    out = lax.map(one_block, (jnp.arange(nb), q_b))
    return from_blocks(out)


def causal_softmax_attention(q, k, v, scale):
    seq_len = q.shape[-2]

    def one_block(args):
        i, q_i = args
        qp, kp = block_positions(i, seq_len)
        s = jnp.einsum('bhqd,bhkd->bhqk', q_i, k).astype(jnp.float32) * scale
        p = jax.nn.softmax(jnp.where(kp <= qp, s, -jnp.inf), axis=-1)
        return jnp.einsum('bhqk,bhkd->bhqd', p.astype(v.dtype), v)

    nb = seq_len // Q_BLOCK
    out = lax.map(one_block, (jnp.arange(nb), to_blocks(q)))
    return from_blocks(out)


def sb_diff_mixer(x, cos_full, sin_full, w_in, w_out, lq1, lk1, lq2, lk2, subln_g, lambda_init):
    b, s, _ = x.shape
    h = x @ w_in
    cuts = list(np.cumsum([SB_WIDTH, SB_WIDTH, SB_WIDTH, DIFF_QK_WIDTH, DIFF_QK_WIDTH]))
    qa, ka, va, qd, kd, vd = jnp.split(h, cuts, axis=-1)

    def heads(t, n, d):
        return t.reshape(b, s, n, d).transpose(0, 2, 1, 3)

    oa = stick_breaking_attention(heads(qa, SB_HEADS, HEAD_DIM),
                                  heads(ka, SB_HEADS, HEAD_DIM),
                                  heads(va, SB_HEADS, HEAD_DIM))
    oa = oa.transpose(0, 2, 1, 3).reshape(b, s, SB_WIDTH)

    def diff_qk(t):
        t = apply_rope(t.reshape(b, s, 2 * DIFF_HEADS, HEAD_DIM), cos_full, sin_full)
        return t.reshape(b, s, DIFF_HEADS, 2, HEAD_DIM).transpose(0, 2, 3, 1, 4)

    lam = (jnp.exp(jnp.sum(lq1.astype(jnp.float32) * lk1.astype(jnp.float32)))
           - jnp.exp(jnp.sum(lq2.astype(jnp.float32) * lk2.astype(jnp.float32)))
           + lambda_init)
    od = differential_attention(diff_qk(qd), diff_qk(kd), heads(vd, DIFF_HEADS, DIFF_V_DIM), lam)
    od = rms_norm(od, subln_g, LN_EPS) * (1.0 - lambda_init)
    od = od.transpose(0, 2, 1, 3).reshape(b, s, DIFF_V_WIDTH)

    return jnp.concatenate([oa, od.astype(oa.dtype)], axis=-1) @ w_out


def mla_mixer(x, cos_rope, sin_rope, w_in, q_norm_g, kv_norm_g, w_q_up, w_kv_up, w_out):
    b, s, _ = x.shape
    h = x @ w_in
    c_q, c_kv, k_pe = jnp.split(h, [MLA_Q_RANK, MLA_Q_RANK + MLA_KV_RANK], axis=-1)
    q = (rms_norm(c_q, q_norm_g, RMS_EPS) @ w_q_up).reshape(b, s, MLA_HEADS, MLA_QK_DIM)
    q_nope, q_pe = jnp.split(q, [MLA_NOPE_DIM], axis=-1)
    q_pe = apply_rope(q_pe, cos_rope, sin_rope)
    kv = (rms_norm(c_kv, kv_norm_g, RMS_EPS) @ w_kv_up).reshape(b, s, MLA_HEADS, MLA_NOPE_DIM + MLA_V_DIM)
    k_nope, v = jnp.split(kv, [MLA_NOPE_DIM], axis=-1)
    k_pe = apply_rope(k_pe[:, :, None, :], cos_rope, sin_rope)
    k_pe = jnp.broadcast_to(k_pe, (b, s, MLA_HEADS, MLA_ROPE_DIM))
    qh = jnp.concatenate([q_nope, q_pe], axis=-1).transpose(0, 2, 1, 3)
    kh = jnp.concatenate([k_nope, k_pe], axis=-1).transpose(0, 2, 1, 3)
    o = causal_softmax_attention(qh, kh, v.transpose(0, 2, 1, 3), MLA_QK_DIM ** -0.5)
    return o.transpose(0, 2, 1, 3).reshape(b, s, MLA_HEADS * MLA_V_DIM) @ w_out


def swiglu_ffn(x, w_gate, w_up, w_down):
    return (jax.nn.silu(x @ w_gate) * (x @ w_up)) @ w_down


def setup_inputs(seed: int = 0) -> dict:
    key = jax.random.key(seed)
    ks = jax.random.split(key, 24)

    def w(k, shape, fan_in, gain=1.0):
        return jax.random.normal(k, shape, jnp.float32) * (gain * fan_in ** -0.5)

    def near_one(k, shape):
        return 1.0 + 0.02 * jax.random.normal(k, shape, jnp.float32)

    offsets = jax.random.randint(ks[1], (BATCH, 1), 0, 4096, dtype=jnp.int32)
    positions = (offsets + jnp.arange(SEQ, dtype=jnp.int32)[None, :]).astype(jnp.int32)

    return {
        "x": jax.random.normal(ks[0], (BATCH, SEQ, D_MODEL), jnp.float32),
        "positions": positions,
        "sb_diff_w_in": w(ks[2], (N_EVEN, D_MODEL, EVEN_IN_WIDTH), D_MODEL),
        "sb_diff_w_out": w(ks[3], (N_EVEN, EVEN_MIX_WIDTH, D_MODEL), EVEN_MIX_WIDTH, DN_BETA),
        "diff_lambda_q1": 0.1 * jax.random.normal(ks[4], (N_EVEN, HEAD_DIM), jnp.float32),
        "diff_lambda_k1": 0.1 * jax.random.normal(ks[5], (N_EVEN, HEAD_DIM), jnp.float32),
        "diff_lambda_q2": 0.1 * jax.random.normal(ks[6], (N_EVEN, HEAD_DIM), jnp.float32),
        "diff_lambda_k2": 0.1 * jax.random.normal(ks[7], (N_EVEN, HEAD_DIM), jnp.float32),
        "diff_subln_g": near_one(ks[8], (N_EVEN, DIFF_V_DIM)),
        "mla_w_in": w(ks[9], (N_ODD, D_MODEL, MLA_IN_WIDTH), D_MODEL),
        "mla_q_norm_g": near_one(ks[10], (N_ODD, MLA_Q_RANK)),
        "mla_kv_norm_g": near_one(ks[11], (N_ODD, MLA_KV_RANK)),
        "mla_w_q_up": w(ks[12], (N_ODD, MLA_Q_RANK, MLA_HEADS * MLA_QK_DIM), MLA_Q_RANK),
        "mla_w_kv_up": w(ks[13], (N_ODD, MLA_KV_RANK, MLA_HEADS * (MLA_NOPE_DIM + MLA_V_DIM)), MLA_KV_RANK),
        "mla_w_out": w(ks[14], (N_ODD, MLA_HEADS * MLA_V_DIM, D_MODEL), MLA_HEADS * MLA_V_DIM, DN_BETA),
        "ffn_w_gate": w(ks[15], (DEPTH, D_MODEL, FFN_DIM), D_MODEL),
        "ffn_w_up": w(ks[16], (DEPTH, D_MODEL, FFN_DIM), D_MODEL),
        "ffn_w_down": w(ks[17], (DEPTH, FFN_DIM, D_MODEL), FFN_DIM, DN_BETA),
        "ln_g": near_one(ks[18], (DEPTH, 2, D_MODEL)),
        "ln_b": 0.02 * jax.random.normal(ks[19], (DEPTH, 2, D_MODEL), jnp.float32),
    }


def reference(x, positions, sb_diff_w_in, sb_diff_w_out, diff_lambda_q1, diff_lambda_k1,
              diff_lambda_q2, diff_lambda_k2, diff_subln_g, mla_w_in, mla_q_norm_g,
              mla_kv_norm_g, mla_w_q_up, mla_w_kv_up, mla_w_out, ffn_w_gate, ffn_w_up,
              ffn_w_down, ln_g, ln_b):
    cos_full, sin_full = rope_tables(positions, HEAD_DIM)
    cos_rope, sin_rope = rope_tables(positions, MLA_ROPE_DIM)
    for layer in range(DEPTH):
        if layer % 2 == 0:
            i = layer // 2
            lambda_init = 0.8 - 0.6 * math.exp(-0.3 * layer)
            mix = sb_diff_mixer(x, cos_full, sin_full, sb_diff_w_in[i], sb_diff_w_out[i],
                                diff_lambda_q1[i], diff_lambda_k1[i], diff_lambda_q2[i],
                                diff_lambda_k2[i], diff_subln_g[i], lambda_init)
        else:
            j = layer // 2
            mix = mla_mixer(x, cos_rope, sin_rope, mla_w_in[j], mla_q_norm_g[j], mla_kv_norm_g[j],
                            mla_w_q_up[j], mla_w_kv_up[j], mla_w_out[j])
        x = layer_norm(DN_ALPHA * x + mix, ln_g[layer, 0], ln_b[layer, 0])
        x = layer_norm(DN_ALPHA * x + swiglu_ffn(x, ffn_w_gate[layer], ffn_w_up[layer], ffn_w_down[layer]),
                       ln_g[layer, 1], ln_b[layer, 1])
    return x
```

```python
import functools
import math

import jax
import jax.numpy as jnp
from jax import lax
from jax.experimental import pallas as pl
from jax.experimental.pallas import tpu as pltpu

F32 = jnp.float32
BF16 = jnp.bfloat16

D_MODEL = 2048
DEPTH = 2
HEAD_DIM = 128
SB_HEADS = 8
DIFF_HEADS = 4
DIFF_V_DIM = 256
SB_WIDTH = SB_HEADS * HEAD_DIM
DIFF_QK_WIDTH = DIFF_HEADS * 2 * HEAD_DIM
DIFF_V_WIDTH = DIFF_HEADS * DIFF_V_DIM
EVEN_IN_WIDTH = 3 * SB_WIDTH + 2 * DIFF_QK_WIDTH + DIFF_V_WIDTH
MLA_HEADS = 16
MLA_Q_RANK = 512
MLA_KV_RANK = 512
MLA_NOPE_DIM = 128
MLA_ROPE_DIM = 64
MLA_V_DIM = 128
MLA_QK_DIM = MLA_NOPE_DIM + MLA_ROPE_DIM
FFN_DIM = 5632
ROPE_THETA = 10000.0
LN_EPS = 1e-5
RMS_EPS = 1e-6
DN_ALPHA = (2 * DEPTH) ** 0.25

LANES = 128
V7X_VMEM_LIMIT = 52 * 1024 * 1024
NEG_BIG = -1e30

_NT = (((1,), (1,)), ((), ()))


def _params(sem):
    return pltpu.CompilerParams(dimension_semantics=sem, vmem_limit_bytes=V7X_VMEM_LIMIT)


def _rope(x, cos, sin_signed):
    return x * cos + pltpu.roll(x, 64, 1) * sin_signed


def _rope_tables_kernel(pos_ref, freq_ref, sign_ref, cos_ref, sin_ref):
    ang = pos_ref[...].astype(F32) * freq_ref[0]
    cos_ref[0] = jnp.cos(ang)
    sin_ref[0] = jnp.sin(ang) * sign_ref[...]


def _rope_tables(pos_col, freqs, sign):
    t = pos_col.shape[0]
    bt = min(t, 1024)
    nf = freqs.shape[0]
    return pl.pallas_call(
        _rope_tables_kernel,
        grid=(nf, t // bt),
        in_specs=[pl.BlockSpec((bt, 1), lambda f, i: (i, 0)),
                  pl.BlockSpec((1, 1, LANES), lambda f, i: (f, 0, 0)),
                  pl.BlockSpec((1, LANES), lambda f, i: (0, 0))],
        out_specs=[pl.BlockSpec((1, bt, LANES), lambda f, i: (f, i, 0)),
                   pl.BlockSpec((1, bt, LANES), lambda f, i: (f, i, 0))],
        out_shape=[jax.ShapeDtypeStruct((nf, t, LANES), F32)] * 2,
        compiler_params=_params(("arbitrary", "arbitrary")),
        name="rope_tables",
    )(pos_col, freqs.reshape(nf, 1, LANES), sign)


def _in_proj0_kernel(x_ref, w_ref, cos_ref, sin_ref, o_ref, *, bn, scale):
    j = pl.program_id(1)
    col = j * bn
    is_rope = (col >= 3 * SB_WIDTH) & (col < 3 * SB_WIDTH + 2 * DIFF_QK_WIDTH)
    is_q = (col < SB_WIDTH) | ((col >= 3 * SB_WIDTH) & (col < 3 * SB_WIDTH + DIFF_QK_WIDTH))
    rf = jnp.where(is_rope, 1.0, 0.0).astype(F32)
    sc = jnp.where(is_q, scale, 1.0).astype(F32)
    a = (cos_ref[0] * rf + (1.0 - rf)) * sc
    b = sin_ref[0] * (rf * sc)
    acc = jnp.dot(x_ref[...], w_ref[...], preferred_element_type=F32)
    for g in range(bn // LANES):
        blk = acc[:, g * LANES:(g + 1) * LANES]
        o_ref[:, g * LANES:(g + 1) * LANES] = _rope(blk, a, b).astype(o_ref.dtype)


def _in_proj0(x, w, cos_tab, sin_tab, *, bm, bn):
    t, k = x.shape
    n = w.shape[1]
    kern = functools.partial(_in_proj0_kernel, bn=bn, scale=HEAD_DIM ** -0.5)
    return pl.pallas_call(
        kern,
        grid=(t // bm, n // bn),
        in_specs=[pl.BlockSpec((bm, k), lambda i, j: (i, 0)),
                  pl.BlockSpec((k, bn), lambda i, j: (0, j)),
                  pl.BlockSpec((1, bm, LANES), lambda i, j: (0, i, 0)),
                  pl.BlockSpec((1, bm, LANES), lambda i, j: (0, i, 0))],
        out_specs=pl.BlockSpec((bm, bn), lambda i, j: (i, j)),
        out_shape=jax.ShapeDtypeStruct((t, n), BF16),
        compiler_params=_params(("parallel", "arbitrary")),
        name="in_proj0",
    )(x, w, cos_tab, sin_tab)


def _mm_kernel(x_ref, w_ref, o_ref):
    o_ref[...] = jnp.dot(x_ref[...], w_ref[...], preferred_element_type=F32).astype(o_ref.dtype)


def _mm(x, w, *, bm, bn, name):
    t, k = x.shape
    n = w.shape[1]
    return pl.pallas_call(
        _mm_kernel,
        grid=(t // bm, n // bn),
        in_specs=[pl.BlockSpec((bm, k), lambda i, j: (i, 0)),
                  pl.BlockSpec((k, bn), lambda i, j: (0, j))],
        out_specs=pl.BlockSpec((bm, bn), lambda i, j: (i, j)),
        out_shape=jax.ShapeDtypeStruct((t, n), BF16),
        compiler_params=_params(("parallel", "arbitrary")),
        name=name,
    )(x, w)


def _mla_q_up_kernel(x_ref, w_ref, cos_ref, sin_ref, o_ref, *, bn, scale):
    acc = jnp.dot(x_ref[...], w_ref[...], preferred_element_type=F32) * scale
    cos = cos_ref[0]
    sin = sin_ref[0]
    for g in range(bn // LANES):
        blk = acc[:, g * LANES:(g + 1) * LANES]
        if g % 2 == 1:
            blk = _rope(blk, cos, sin)
        o_ref[:, g * LANES:(g + 1) * LANES] = blk.astype(o_ref.dtype)


def _mla_q_up(x, w, cos_tab, sin_tab, *, bm, bn):
    t, k = x.shape
    n = w.shape[1]
    kern = functools.partial(_mla_q_up_kernel, bn=bn, scale=MLA_QK_DIM ** -0.5)
    return pl.pallas_call(
        kern,
        grid=(t // bm, n // bn),
        in_specs=[pl.BlockSpec((bm, k), lambda i, j: (i, 0)),
                  pl.BlockSpec((k, bn), lambda i, j: (0, j)),
                  pl.BlockSpec((1, bm, LANES), lambda i, j: (1, i, 0)),
                  pl.BlockSpec((1, bm, LANES), lambda i, j: (1, i, 0))],
        out_specs=pl.BlockSpec((bm, bn), lambda i, j: (i, j)),
        out_shape=jax.ShapeDtypeStruct((t, n), BF16),
        compiler_params=_params(("parallel", "arbitrary")),
        name="mla_q_up",
    )(x, w, cos_tab, sin_tab)


def _rms(x, g, eps):
    return x * lax.rsqrt(jnp.mean(x * x, axis=-1, keepdims=True) + eps) * g


def _mla_in_kernel(x_ref, w_ref, gq_ref, gkv_ref, cos_ref, sin_ref, cq_ref, ckv_ref, kpe_ref):
    acc = jnp.dot(x_ref[...], w_ref[...], preferred_element_type=F32)
    cq_ref[...] = _rms(acc[:, :MLA_Q_RANK], gq_ref[...], RMS_EPS).astype(cq_ref.dtype)
    ckv_ref[...] = _rms(acc[:, MLA_Q_RANK:MLA_Q_RANK + MLA_KV_RANK], gkv_ref[...],
                        RMS_EPS).astype(ckv_ref.dtype)
    kpe = acc[:, MLA_Q_RANK + MLA_KV_RANK:]
    kpe_ref[...] = _rope(kpe, cos_ref[0], sin_ref[0]).astype(kpe_ref.dtype)


def _mla_in(x, w, gq, gkv, cos_tab, sin_tab, *, bm):
    t, k = x.shape
    n = w.shape[1]
    row = lambda i: (i, 0)
    fixed = lambda i: (0, 0)
    return pl.pallas_call(
        _mla_in_kernel,
        grid=(t // bm,),
        in_specs=[pl.BlockSpec((bm, k), row),
                  pl.BlockSpec((k, n), fixed),
                  pl.BlockSpec((1, MLA_Q_RANK), fixed),
                  pl.BlockSpec((1, MLA_KV_RANK), fixed),
                  pl.BlockSpec((1, bm, LANES), lambda i: (1, i, 0)),
                  pl.BlockSpec((1, bm, LANES), lambda i: (1, i, 0))],
        out_specs=[pl.BlockSpec((bm, MLA_Q_RANK), row),
                   pl.BlockSpec((bm, MLA_KV_RANK), row),
                   pl.BlockSpec((bm, LANES), row)],
        out_shape=[jax.ShapeDtypeStruct((t, MLA_Q_RANK), BF16),
                   jax.ShapeDtypeStruct((t, MLA_KV_RANK), BF16),
                   jax.ShapeDtypeStruct((t, LANES), BF16)],
        compiler_params=_params(("parallel",)),
        name="mla_in",
    )(x, w, gq, gkv, cos_tab, sin_tab)


def _mm_res_ln_kernel(a_ref, w_ref, res_ref, g_ref, b_ref, *rest, nk, want_bf16):
    if want_bf16:
        of_ref, ob_ref = rest[0], rest[1]
        scratch = rest[2:]
    else:
        of_ref, ob_ref = rest[0], None
        scratch = rest[1:]
    part = jnp.dot(a_ref[...], w_ref[...], preferred_element_type=F32)

    def finish(acc):
        y = DN_ALPHA * res_ref[...] + acc
        mu = jnp.mean(y, axis=-1, keepdims=True)
        d = y - mu
        var = jnp.mean(d * d, axis=-1, keepdims=True)
        out = d * lax.rsqrt(var + LN_EPS) * g_ref[...] + b_ref[...]
        of_ref[...] = out
        if ob_ref is not None:
            ob_ref[...] = out.astype(ob_ref.dtype)

    if nk == 1:
        finish(part)
        return
    acc_ref = scratch[0]
    k = pl.program_id(1)

    @pl.when(k == 0)
    def _():
        acc_ref[...] = part

    @pl.when((k > 0) & (k < nk - 1))
    def _():
        acc_ref[...] += part

    @pl.when(k == nk - 1)
    def _():
        finish(acc_ref[...] + part)


def _mm_res_ln(a, w, res, g, b, *, bm, bk, want_bf16, name):
    t, kdim = a.shape
    n = w.shape[1]
    nk = kdim // bk
    kern = functools.partial(_mm_res_ln_kernel, nk=nk, want_bf16=want_bf16)
    row = lambda i, k: (i, 0)
    out_specs = [pl.BlockSpec((bm, n), row)]
    out_shape = [jax.ShapeDtypeStruct((t, n), F32)]
    if want_bf16:
        out_specs.append(pl.BlockSpec((bm, n), row))
        out_shape.append(jax.ShapeDtypeStruct((t, n), BF16))
    outs = pl.pallas_call(
        kern,
        grid=(t // bm, nk),
        in_specs=[pl.BlockSpec((bm, bk), lambda i, k: (i, k)),
                  pl.BlockSpec((bk, n), lambda i, k: (k, 0)),
                  pl.BlockSpec((bm, n), row),
                  pl.BlockSpec((1, n), lambda i, k: (0, 0)),
                  pl.BlockSpec((1, n), lambda i, k: (0, 0))],
        out_specs=out_specs,
        out_shape=out_shape,
        scratch_shapes=[pltpu.VMEM((bm, n), F32)] if nk > 1 else [],
        compiler_params=_params(("parallel", "arbitrary")),
        name=name,
    )(a, w, res, g, b)
    return outs if want_bf16 else (outs[0], None)


def _ffn_up_kernel(x_ref, wg_ref, wu_ref, o_ref):
    x = x_ref[...]
    gate = jnp.dot(x, wg_ref[...], preferred_element_type=F32)
    up = jnp.dot(x, wu_ref[...], preferred_element_type=F32)
    o_ref[...] = (gate * jax.nn.sigmoid(gate) * up).astype(o_ref.dtype)


def _ffn_up(x, wg, wu, *, bm, bn):
    t, k = x.shape
    n = wg.shape[1]
    return pl.pallas_call(
        _ffn_up_kernel,
        grid=(t // bm, n // bn),
        in_specs=[pl.BlockSpec((bm, k), lambda i, j: (i, 0)),
                  pl.BlockSpec((k, bn), lambda i, j: (0, j)),
                  pl.BlockSpec((k, bn), lambda i, j: (0, j))],
        out_specs=pl.BlockSpec((bm, bn), lambda i, j: (i, j)),
        out_shape=jax.ShapeDtypeStruct((t, n), BF16),
        compiler_params=_params(("parallel", "arbitrary")),
        name="ffn_up",
    )(x, wg, wu)


def _sb_kernel(q_ref, k_ref, v_ref, u_ref, o_ref, acc_ref, *, tq, sub):
    i = pl.program_id(2)
    q = q_ref[...]
    u = u_ref[...]
    row = lax.broadcasted_iota(jnp.int32, (tq, tq), 0)
    col = lax.broadcasted_iota(jnp.int32, (tq, tq), 1)
    strict = col < row

    def tile(j, carry, diag):
        start = pl.multiple_of(j * tq, tq)
        ks = k_ref[pl.ds(start, tq), :]
        vs = v_ref[pl.ds(start, tq), :]
        z = lax.dot_general(q, ks, _NT, preferred_element_type=F32)
        log_beta = jnp.minimum(z, 0.0) - jnp.log(1.0 + jnp.exp(-jnp.abs(z)))
        log_rest = log_beta - z
        if diag:
            log_rest = jnp.where(strict, log_rest, 0.0)
        ws = [None] * (tq // sub)
        for sb in reversed(range(tq // sub)):
            sl = slice(sb * sub, (sb + 1) * sub)
            l = log_rest[:, sl]
            hi = l.astype(BF16)
            lo = (l - hi.astype(F32)).astype(BF16)
            later = (jnp.dot(hi, u, preferred_element_type=F32)
                     + jnp.dot(lo, u, preferred_element_type=F32) + carry)
            w = jnp.exp(log_beta[:, sl] + later)
            if diag:
                w = jnp.where(strict[:, sl], w, 0.0)
            ws[sb] = w.astype(BF16)
            carry = carry + jnp.sum(l, axis=-1, keepdims=True)
        w_full = jnp.concatenate(ws, axis=-1)
        acc_ref[...] += jnp.dot(w_full, vs, preferred_element_type=F32)
        return carry

    acc_ref[...] = jnp.zeros_like(acc_ref)
    carry = tile(i, jnp.zeros((tq, 1), F32), True)
    lax.fori_loop(0, i, lambda s, c: tile(i - 1 - s, c, False), carry)
    o_ref[...] = acc_ref[...].astype(o_ref.dtype)


def _sb_attention(h, u, *, batch, seq, tq, sub):
    nq = seq // tq
    nh = SB_HEADS
    kern = functools.partial(_sb_kernel, tq=tq, sub=sub)
    kcol = SB_WIDTH // HEAD_DIM
    return pl.pallas_call(
        kern,
        grid=(batch, nh, nq),
        in_specs=[pl.BlockSpec((tq, HEAD_DIM), lambda b, hh, i: (b * nq + i, hh)),
                  pl.BlockSpec((seq, HEAD_DIM), lambda b, hh, i: (b, kcol + hh)),
                  pl.BlockSpec((seq, HEAD_DIM), lambda b, hh, i: (b, 2 * kcol + hh)),
                  pl.BlockSpec((sub, sub), lambda b, hh, i: (0, 0))],
        out_specs=pl.BlockSpec((tq, HEAD_DIM), lambda b, hh, i: (b * nq + i, hh)),
        out_shape=jax.ShapeDtypeStruct((batch * seq, SB_WIDTH), BF16),
        scratch_shapes=[pltpu.VMEM((tq, HEAD_DIM), F32)],
        compiler_params=_params(("parallel", "parallel", "arbitrary")),
        name="sb_attention",
    )(h, h, h, u)


def _softmax_step(s, vs, m_ref, l_ref, acc_ref):
    m_old = m_ref[...]
    m_new = jnp.maximum(m_old, jnp.max(s, axis=-1, keepdims=True))
    alpha = jnp.exp(m_old - m_new)
    p = jnp.exp(s - m_new)
    l_ref[...] = alpha * l_ref[...] + jnp.sum(p, axis=-1, keepdims=True)
    acc_ref[...] = alpha * acc_ref[...] + jnp.dot(p.astype(BF16), vs, preferred_element_type=F32)
    m_ref[...] = m_new


def _diff_kernel(q_ref, k_ref, v_ref, lq1_ref, lk1_ref, lq2_ref, lk2_ref, g_ref, o_ref,
                 acc_ref, m_ref, l_ref, *, tq, lambda_init):
    i = pl.program_id(2)
    q = q_ref[...]
    row = lax.broadcasted_iota(jnp.int32, (tq, tq), 0)
    col = lax.broadcasted_iota(jnp.int32, (tq, tq), 1)
    causal = col <= row
    m_ref[...] = jnp.full_like(m_ref, NEG_BIG)
    l_ref[...] = jnp.zeros_like(l_ref)
    acc_ref[...] = jnp.zeros_like(acc_ref)

    def tile(j, diag):
        start = pl.multiple_of(j * tq, tq)
        ks = k_ref[pl.ds(start, tq), :]
        vs = v_ref[pl.ds(start, tq), :]
        for mp in range(2):
            sl = slice(mp * HEAD_DIM, (mp + 1) * HEAD_DIM)
            s = lax.dot_general(q[:, sl], ks[:, sl], _NT, preferred_element_type=F32)
            if diag:
                s = jnp.where(causal, s, NEG_BIG)
            _softmax_step(s, vs, m_ref.at[mp], l_ref.at[mp], acc_ref.at[mp])

    def body(j, c):
        tile(j, False)
        return c

    lax.fori_loop(0, i, body, 0)
    tile(i, True)

    lam = (jnp.exp(jnp.sum(lq1_ref[...] * lk1_ref[...], axis=-1, keepdims=True))
           - jnp.exp(jnp.sum(lq2_ref[...] * lk2_ref[...], axis=-1, keepdims=True))
           + lambda_init)
    o = acc_ref[0] / l_ref[0] - lam * (acc_ref[1] / l_ref[1])
    o = _rms(o, g_ref[...], LN_EPS) * (1.0 - lambda_init)
    o_ref[...] = o.astype(o_ref.dtype)


def _diff_attention(h, lq1, lk1, lq2, lk2, g, *, batch, seq, tq, lambda_init):
    nq = seq // tq
    kern = functools.partial(_diff_kernel, tq=tq, lambda_init=lambda_init)
    qcol = 3 * SB_WIDTH // DIFF_V_DIM
    kcol = qcol + DIFF_QK_WIDTH // DIFF_V_DIM
    vcol = kcol + DIFF_QK_WIDTH // DIFF_V_DIM
    vec = pl.BlockSpec((1, HEAD_DIM), lambda b, hh, i: (0, 0))
    return pl.pallas_call(
        kern,
        grid=(batch, DIFF_HEADS, nq),
        in_specs=[pl.BlockSpec((tq, DIFF_V_DIM), lambda b, hh, i: (b * nq + i, qcol + hh)),
                  pl.BlockSpec((seq, DIFF_V_DIM), lambda b, hh, i: (b, kcol + hh)),
                  pl.BlockSpec((seq, DIFF_V_DIM), lambda b, hh, i: (b, vcol + hh)),
                  vec, vec, vec, vec,
                  pl.BlockSpec((1, DIFF_V_DIM), lambda b, hh, i: (0, 0))],
        out_specs=pl.BlockSpec((tq, DIFF_V_DIM), lambda b, hh, i: (b * nq + i, hh)),
        out_shape=jax.ShapeDtypeStruct((batch * seq, DIFF_V_WIDTH), BF16),
        scratch_shapes=[pltpu.VMEM((2, tq, DIFF_V_DIM), F32),
                        pltpu.VMEM((2, tq, 1), F32),
                        pltpu.VMEM((2, tq, 1), F32)],
        compiler_params=_params(("parallel", "parallel", "arbitrary")),
        name="diff_attention",
    )(h, h, h, lq1, lk1, lq2, lk2, g)


def _mla_kernel(q_ref, kn_ref, kpe_ref, v_ref, o_ref, acc_ref, m_ref, l_ref, *, tq):
    i = pl.program_id(2)
    q = q_ref[...]
    row = lax.broadcasted_iota(jnp.int32, (tq, tq), 0)
    col = lax.broadcasted_iota(jnp.int32, (tq, tq), 1)
    causal = col <= row
    m_ref[...] = jnp.full_like(m_ref, NEG_BIG)
    l_ref[...] = jnp.zeros_like(l_ref)
    acc_ref[...] = jnp.zeros_like(acc_ref)

    def tile(j, diag):
        start = pl.multiple_of(j * tq, tq)
        ks = jnp.concatenate([kn_ref[pl.ds(start, tq), :], kpe_ref[pl.ds(start, tq), :]], axis=-1)
        s = lax.dot_general(q, ks, _NT, preferred_element_type=F32)
        if diag:
            s = jnp.where(causal, s, NEG_BIG)
        _softmax_step(s, v_ref[pl.ds(start, tq), :], m_ref, l_ref, acc_ref)

    def body(j, c):
        tile(j, False)
        return c

    lax.fori_loop(0, i, body, 0)
    tile(i, True)
    o_ref[...] = (acc_ref[...] / l_ref[...]).astype(o_ref.dtype)


def _mla_attention(q, kv, kpe, *, batch, seq, tq):
    nq = seq // tq
    kern = functools.partial(_mla_kernel, tq=tq)
    return pl.pallas_call(
        kern,
        grid=(batch, MLA_HEADS, nq),
        in_specs=[pl.BlockSpec((tq, 2 * LANES), lambda b, hh, i: (b * nq + i, hh)),
                  pl.BlockSpec((seq, LANES), lambda b, hh, i: (b, 2 * hh)),
                  pl.BlockSpec((seq, LANES), lambda b, hh, i: (b, 0)),
                  pl.BlockSpec((seq, LANES), lambda b, hh, i: (b, 2 * hh + 1))],
        out_specs=pl.BlockSpec((tq, MLA_V_DIM), lambda b, hh, i: (b * nq + i, hh)),
        out_shape=jax.ShapeDtypeStruct((batch * seq, MLA_HEADS * MLA_V_DIM), BF16),
        scratch_shapes=[pltpu.VMEM((tq, MLA_V_DIM), F32),
                        pltpu.VMEM((tq, 1), F32),
                        pltpu.VMEM((tq, 1), F32)],
        compiler_params=_params(("parallel", "parallel", "arbitrary")),
        name="mla_attention",
    )(q, kv, kpe, kv)


def _spread_rope_cols(w):
    half = MLA_ROPE_DIM // 2
    z = jnp.zeros(w.shape[:-1] + (LANES // 2 - half,), w.dtype)
    return jnp.concatenate([w[..., :half], z, w[..., half:], z], axis=-1)


def _rope_freqs():
    f_full = ROPE_THETA ** (-jnp.arange(0, HEAD_DIM, 2, dtype=F32) / HEAD_DIM)
    f_mla = ROPE_THETA ** (-jnp.arange(0, MLA_ROPE_DIM, 2, dtype=F32) / MLA_ROPE_DIM)
    row_full = jnp.concatenate([f_full, f_full])
    row_mla = _spread_rope_cols(jnp.concatenate([f_mla, f_mla]))
    sign = jnp.concatenate([-jnp.ones((1, LANES // 2), F32), jnp.ones((1, LANES // 2), F32)], axis=-1)
    return jnp.stack([row_full, row_mla]), sign


def kernel(x, positions, sb_diff_w_in, sb_diff_w_out, diff_lambda_q1, diff_lambda_k1,
           diff_lambda_q2, diff_lambda_k2, diff_subln_g, mla_w_in, mla_q_norm_g, mla_kv_norm_g,
           mla_w_q_up, mla_w_kv_up, mla_w_out, ffn_w_gate, ffn_w_up, ffn_w_down, ln_g, ln_b):
    batch, seq, d = x.shape
    t = batch * seq
    tq = min(seq, 512)
    sub = 256
    bm = min(t, 1024)
    bm_ln = min(t, 512)

    freqs, sign = _rope_freqs()
    cos_tab, sin_tab = _rope_tables(positions.reshape(t, 1), freqs, sign)
    u = (lax.broadcasted_iota(jnp.int32, (sub, sub), 0)
         > lax.broadcasted_iota(jnp.int32, (sub, sub), 1)).astype(BF16)

    xf = x.reshape(t, d)
    xb = xf.astype(BF16)

    def ffn(layer, xf, xb, want_bf16):
        h = _ffn_up(xb, ffn_w_gate[layer].astype(BF16), ffn_w_up[layer].astype(BF16), bm=bm, bn=512)
        return _mm_res_ln(h, ffn_w_down[layer].astype(BF16), xf, ln_g[layer, 1][None], ln_b[layer, 1][None],
                          bm=bm_ln, bk=FFN_DIM // 4, want_bf16=want_bf16, name=f"ffn_down{layer}")

    h0 = _in_proj0(xb, sb_diff_w_in[0].astype(BF16), cos_tab, sin_tab, bm=bm, bn=512)
    oa = _sb_attention(h0, u, batch=batch, seq=seq, tq=tq, sub=sub)
    lambda_init = 0.8 - 0.6 * math.exp(-0.3 * 0)
    od = _diff_attention(h0, diff_lambda_q1[0][None], diff_lambda_k1[0][None], diff_lambda_q2[0][None],
                         diff_lambda_k2[0][None], diff_subln_g[0][None],
                         batch=batch, seq=seq, tq=tq, lambda_init=lambda_init)
    mix = jnp.concatenate([oa, od], axis=-1)
    xf, xb = _mm_res_ln(mix, sb_diff_w_out[0].astype(BF16), xf, ln_g[0, 0][None], ln_b[0, 0][None],
                        bm=bm_ln, bk=d, want_bf16=True, name="attn_out0")
    xf, xb = ffn(0, xf, xb, True)

    w_in1 = jnp.concatenate([mla_w_in[0][:, :MLA_Q_RANK + MLA_KV_RANK],
                             _spread_rope_cols(mla_w_in[0][:, MLA_Q_RANK + MLA_KV_RANK:])],
                            axis=-1).astype(BF16)
    wq = mla_w_q_up[0].reshape(MLA_Q_RANK, MLA_HEADS, MLA_QK_DIM)
    wq = jnp.concatenate([wq[..., :MLA_NOPE_DIM], _spread_rope_cols(wq[..., MLA_NOPE_DIM:])], axis=-1)
    wq = wq.reshape(MLA_Q_RANK, MLA_HEADS * 2 * LANES).astype(BF16)
    cq, ckv, kpe = _mla_in(xb, w_in1, mla_q_norm_g[0][None], mla_kv_norm_g[0][None], cos_tab, sin_tab,
                           bm=bm_ln)
    q1 = _mla_q_up(cq, wq, cos_tab, sin_tab, bm=bm, bn=1024)
    kv1 = _mm(ckv, mla_w_kv_up[0].astype(BF16), bm=bm, bn=1024, name="mla_kv_up")
    o1 = _mla_attention(q1, kv1, kpe, batch=batch, seq=seq, tq=tq)
    xf, xb = _mm_res_ln(o1, mla_w_out[0].astype(BF16), xf, ln_g[1, 0][None], ln_b[1, 0][None],
                        bm=bm_ln, bk=d, want_bf16=True, name="attn_out1")
    xf, _ = ffn(1, xf, xb, False)
    return xf.reshape(batch, seq, d)
```

```python
import functools
import math

import jax
import jax.numpy as jnp
from jax import lax
from jax.experimental import pallas as pl
from jax.experimental.pallas import tpu as pltpu

F32 = jnp.float32
BF16 = jnp.bfloat16

D_MODEL = 2048
DEPTH = 2
HEAD_DIM = 128
SB_HEADS = 8
DIFF_HEADS = 4
DIFF_V_DIM = 256
SB_WIDTH = SB_HEADS * HEAD_DIM
DIFF_QK_WIDTH = DIFF_HEADS * 2 * HEAD_DIM
DIFF_V_WIDTH = DIFF_HEADS * DIFF_V_DIM
EVEN_IN_WIDTH = 3 * SB_WIDTH + 2 * DIFF_QK_WIDTH + DIFF_V_WIDTH
MLA_HEADS = 16
MLA_Q_RANK = 512
MLA_KV_RANK = 512
MLA_NOPE_DIM = 128
MLA_ROPE_DIM = 64
MLA_V_DIM = 128
MLA_QK_DIM = MLA_NOPE_DIM + MLA_ROPE_DIM
FFN_DIM = 5632
ROPE_THETA = 10000.0
LN_EPS = 1e-5
RMS_EPS = 1e-6
DN_ALPHA = (2 * DEPTH) ** 0.25

LANES = 128
V7X_VMEM_LIMIT = 52 * 1024 * 1024
NEG_BIG = -1e30

_NT = (((1,), (1,)), ((), ()))


def _params(sem):
    return pltpu.CompilerParams(dimension_semantics=sem, vmem_limit_bytes=V7X_VMEM_LIMIT)


def _rope(x, cos, sin_signed):
    return x * cos + pltpu.roll(x, 64, 1) * sin_signed


def _rope_tables_kernel(pos_ref, freq_ref, sign_ref, cos_ref, sin_ref):
    ang = pos_ref[...].astype(F32) * freq_ref[0]
    cos_ref[0] = jnp.cos(ang)
    sin_ref[0] = jnp.sin(ang) * sign_ref[...]


def _rope_tables(pos_col, freqs, sign):
    t = pos_col.shape[0]
    bt = min(t, 1024)
    nf = freqs.shape[0]
    return pl.pallas_call(
        _rope_tables_kernel,
        grid=(nf, t // bt),
        in_specs=[pl.BlockSpec((bt, 1), lambda f, i: (i, 0)),
                  pl.BlockSpec((1, 1, LANES), lambda f, i: (f, 0, 0)),
                  pl.BlockSpec((1, LANES), lambda f, i: (0, 0))],
        out_specs=[pl.BlockSpec((1, bt, LANES), lambda f, i: (f, i, 0)),
                   pl.BlockSpec((1, bt, LANES), lambda f, i: (f, i, 0))],
        out_shape=[jax.ShapeDtypeStruct((nf, t, LANES), F32)] * 2,
        compiler_params=_params(("arbitrary", "arbitrary")),
        name="rope_tables",
    )(pos_col, freqs.reshape(nf, 1, LANES), sign)


def _in_proj0_kernel(x_ref, w_ref, cos_ref, sin_ref, o_ref, *, bn, scale):
    j = pl.program_id(1)
    col = j * bn
    is_rope = (col >= 3 * SB_WIDTH) & (col < 3 * SB_WIDTH + 2 * DIFF_QK_WIDTH)
    is_q = (col < SB_WIDTH) | ((col >= 3 * SB_WIDTH) & (col < 3 * SB_WIDTH + DIFF_QK_WIDTH))
    rf = jnp.where(is_rope, 1.0, 0.0).astype(F32)
    sc = jnp.where(is_q, scale, 1.0).astype(F32)
    a = (cos_ref[0] * rf + (1.0 - rf)) * sc
    b = sin_ref[0] * (rf * sc)
    acc = jnp.dot(x_ref[...], w_ref[...], preferred_element_type=F32)
    for g in range(bn // LANES):
        blk = acc[:, g * LANES:(g + 1) * LANES]
        o_ref[:, g * LANES:(g + 1) * LANES] = _rope(blk, a, b).astype(o_ref.dtype)


def _in_proj0(x, w, cos_tab, sin_tab, *, bm, bn):
    t, k = x.shape
    n = w.shape[1]
    kern = functools.partial(_in_proj0_kernel, bn=bn, scale=HEAD_DIM ** -0.5)
    return pl.pallas_call(
        kern,
        grid=(t // bm, n // bn),
        in_specs=[pl.BlockSpec((bm, k), lambda i, j: (i, 0)),
                  pl.BlockSpec((k, bn), lambda i, j: (0, j)),
                  pl.BlockSpec((1, bm, LANES), lambda i, j: (0, i, 0)),
                  pl.BlockSpec((1, bm, LANES), lambda i, j: (0, i, 0))],
        out_specs=pl.BlockSpec((bm, bn), lambda i, j: (i, j)),
        out_shape=jax.ShapeDtypeStruct((t, n), BF16),
        compiler_params=_params(("parallel", "arbitrary")),
        name="in_proj0",
    )(x, w, cos_tab, sin_tab)


def _mm_kernel(x_ref, w_ref, o_ref):
    o_ref[...] = jnp.dot(x_ref[...], w_ref[...], preferred_element_type=F32).astype(o_ref.dtype)


def _mm(x, w, *, bm, bn, name):
    t, k = x.shape
    n = w.shape[1]
    return pl.pallas_call(
        _mm_kernel,
        grid=(t // bm, n // bn),
        in_specs=[pl.BlockSpec((bm, k), lambda i, j: (i, 0)),
                  pl.BlockSpec((k, bn), lambda i, j: (0, j))],
        out_specs=pl.BlockSpec((bm, bn), lambda i, j: (i, j)),
        out_shape=jax.ShapeDtypeStruct((t, n), BF16),
        compiler_params=_params(("parallel", "arbitrary")),
        name=name,
    )(x, w)


def _mla_q_up_kernel(x_ref, w_ref, cos_ref, sin_ref, o_ref, *, bn, scale):
    acc = jnp.dot(x_ref[...], w_ref[...], preferred_element_type=F32) * scale
    cos = cos_ref[0]
    sin = sin_ref[0]
    for g in range(bn // LANES):
        blk = acc[:, g * LANES:(g + 1) * LANES]
        if g % 2 == 1:
            blk = _rope(blk, cos, sin)
        o_ref[:, g * LANES:(g + 1) * LANES] = blk.astype(o_ref.dtype)


def _mla_q_up(x, w, cos_tab, sin_tab, *, bm, bn):
    t, k = x.shape
    n = w.shape[1]
    kern = functools.partial(_mla_q_up_kernel, bn=bn, scale=MLA_QK_DIM ** -0.5)
    return pl.pallas_call(
        kern,
        grid=(t // bm, n // bn),
        in_specs=[pl.BlockSpec((bm, k), lambda i, j: (i, 0)),
                  pl.BlockSpec((k, bn), lambda i, j: (0, j)),
                  pl.BlockSpec((1, bm, LANES), lambda i, j: (1, i, 0)),
                  pl.BlockSpec((1, bm, LANES), lambda i, j: (1, i, 0))],
        out_specs=pl.BlockSpec((bm, bn), lambda i, j: (i, j)),
        out_shape=jax.ShapeDtypeStruct((t, n), BF16),
        compiler_params=_params(("parallel", "arbitrary")),
        name="mla_q_up",
    )(x, w, cos_tab, sin_tab)


def _rms(x, g, eps):
    return x * lax.rsqrt(jnp.mean(x * x, axis=-1, keepdims=True) + eps) * g


def _mla_in_kernel(x_ref, w_ref, gq_ref, gkv_ref, cos_ref, sin_ref, cq_ref, ckv_ref, kpe_ref):
    acc = jnp.dot(x_ref[...], w_ref[...], preferred_element_type=F32)
    cq_ref[...] = _rms(acc[:, :MLA_Q_RANK], gq_ref[...], RMS_EPS).astype(cq_ref.dtype)
    ckv_ref[...] = _rms(acc[:, MLA_Q_RANK:MLA_Q_RANK + MLA_KV_RANK], gkv_ref[...],
                        RMS_EPS).astype(ckv_ref.dtype)
    kpe = acc[:, MLA_Q_RANK + MLA_KV_RANK:]
    kpe_ref[...] = _rope(kpe, cos_ref[0], sin_ref[0]).astype(kpe_ref.dtype)


def _mla_in(x, w, gq, gkv, cos_tab, sin_tab, *, bm):
    t, k = x.shape
    n = w.shape[1]
    row = lambda i: (i, 0)
    fixed = lambda i: (0, 0)
    return pl.pallas_call(
        _mla_in_kernel,
        grid=(t // bm,),
        in_specs=[pl.BlockSpec((bm, k), row),
                  pl.BlockSpec((k, n), fixed),
                  pl.BlockSpec((1, MLA_Q_RANK), fixed),
                  pl.BlockSpec((1, MLA_KV_RANK), fixed),
                  pl.BlockSpec((1, bm, LANES), lambda i: (1, i, 0)),
                  pl.BlockSpec((1, bm, LANES), lambda i: (1, i, 0))],
        out_specs=[pl.BlockSpec((bm, MLA_Q_RANK), row),
                   pl.BlockSpec((bm, MLA_KV_RANK), row),
                   pl.BlockSpec((bm, LANES), row)],
        out_shape=[jax.ShapeDtypeStruct((t, MLA_Q_RANK), BF16),
                   jax.ShapeDtypeStruct((t, MLA_KV_RANK), BF16),
                   jax.ShapeDtypeStruct((t, LANES), BF16)],
        compiler_params=_params(("parallel",)),
        name="mla_in",
    )(x, w, gq, gkv, cos_tab, sin_tab)


def _mm_res_ln_kernel(a_ref, w_ref, res_ref, g_ref, b_ref, *rest, nk, want_bf16):
    if want_bf16:
        of_ref, ob_ref = rest[0], rest[1]
        scratch = rest[2:]
    else:
        of_ref, ob_ref = rest[0], None
        scratch = rest[1:]
    part = jnp.dot(a_ref[...], w_ref[...], preferred_element_type=F32)

    def finish(acc):
        y = DN_ALPHA * res_ref[...] + acc
        mu = jnp.mean(y, axis=-1, keepdims=True)
        d = y - mu
        var = jnp.mean(d * d, axis=-1, keepdims=True)
        out = d * lax.rsqrt(var + LN_EPS) * g_ref[...] + b_ref[...]
        of_ref[...] = out
        if ob_ref is not None:
            ob_ref[...] = out.astype(ob_ref.dtype)

    if nk == 1:
        finish(part)
        return
    acc_ref = scratch[0]
    k = pl.program_id(1)

    @pl.when(k == 0)
    def _():
        acc_ref[...] = part

    @pl.when((k > 0) & (k < nk - 1))
    def _():
        acc_ref[...] += part

    @pl.when(k == nk - 1)
    def _():
        finish(acc_ref[...] + part)


def _mm_res_ln(a, w, res, g, b, *, bm, bk, want_bf16, name):
    t, kdim = a.shape
    n = w.shape[1]
    nk = kdim // bk
    kern = functools.partial(_mm_res_ln_kernel, nk=nk, want_bf16=want_bf16)
    row = lambda i, k: (i, 0)
    out_specs = [pl.BlockSpec((bm, n), row)]
    out_shape = [jax.ShapeDtypeStruct((t, n), F32)]
    if want_bf16:
        out_specs.append(pl.BlockSpec((bm, n), row))
        out_shape.append(jax.ShapeDtypeStruct((t, n), BF16))
    outs = pl.pallas_call(
        kern,
        grid=(t // bm, nk),
        in_specs=[pl.BlockSpec((bm, bk), lambda i, k: (i, k)),
                  pl.BlockSpec((bk, n), lambda i, k: (k, 0)),
                  pl.BlockSpec((bm, n), row),
                  pl.BlockSpec((1, n), lambda i, k: (0, 0)),
                  pl.BlockSpec((1, n), lambda i, k: (0, 0))],
        out_specs=out_specs,
        out_shape=out_shape,
        scratch_shapes=[pltpu.VMEM((bm, n), F32)] if nk > 1 else [],
        compiler_params=_params(("parallel", "arbitrary")),
        name=name,
    )(a, w, res, g, b)
    return outs if want_bf16 else (outs[0], None)


def _ffn_up_kernel(x_ref, wg_ref, wu_ref, o_ref):
    x = x_ref[...]
    gate = jnp.dot(x, wg_ref[...], preferred_element_type=F32)
    up = jnp.dot(x, wu_ref[...], preferred_element_type=F32)
    o_ref[...] = (gate * jax.nn.sigmoid(gate) * up).astype(o_ref.dtype)


def _ffn_up(x, wg, wu, *, bm, bn):
    t, k = x.shape
    n = wg.shape[1]
    return pl.pallas_call(
        _ffn_up_kernel,
        grid=(t // bm, n // bn),
        in_specs=[pl.BlockSpec((bm, k), lambda i, j: (i, 0)),
                  pl.BlockSpec((k, bn), lambda i, j: (0, j)),
                  pl.BlockSpec((k, bn), lambda i, j: (0, j))],
        out_specs=pl.BlockSpec((bm, bn), lambda i, j: (i, j)),
        out_shape=jax.ShapeDtypeStruct((t, n), BF16),
        compiler_params=_params(("parallel", "arbitrary")),
        name="ffn_up",
    )(x, wg, wu)


def _sb_kernel(q_ref, k_ref, v_ref, u_ref, o_ref, acc_ref, c_ref, za_ref, zb_ref, *, tq):
    tk = tq // 2
    i = pl.program_id(2)
    q = q_ref[...]
    u = u_ref[...]
    acc_ref[...] = jnp.zeros_like(acc_ref)
    c_ref[...] = jnp.zeros_like(c_ref)

    def logits(t):
        start = pl.multiple_of(t * tk, tk)
        return lax.dot_general(q, k_ref[pl.ds(start, tk), :], _NT, preferred_element_type=F32)

    def step(z, t, mask):
        start = pl.multiple_of(t * tk, tk)
        log_beta = jnp.minimum(z, 0.0) - jnp.log(1.0 + jnp.exp(-jnp.abs(z)))
        log_rest = log_beta - z
        if mask is not None:
            log_rest = jnp.where(mask, log_rest, 0.0)
        hi = log_rest.astype(BF16)
        lo = (log_rest - hi.astype(F32)).astype(BF16)
        later = jnp.dot(hi, u, preferred_element_type=F32) + jnp.dot(lo, u, preferred_element_type=F32)
        carry = c_ref[...]
        ws = []
        for c in range(tk // LANES):
            sl = slice(c * LANES, (c + 1) * LANES)
            w = jnp.exp(log_beta[:, sl] + later[:, sl] + carry)
            if mask is not None:
                w = jnp.where(mask[:, sl], w, 0.0)
            ws.append(w.astype(BF16))
        acc_ref[...] += jnp.dot(jnp.concatenate(ws, axis=-1), v_ref[pl.ds(start, tk), :],
                                preferred_element_type=F32)
        c_ref[...] = carry + jnp.sum(log_rest, axis=-1, keepdims=True)

    row = lax.broadcasted_iota(jnp.int32, (tq, tk), 0)
    col = lax.broadcasted_iota(jnp.int32, (tq, tk), 1)
    t = 2 * i + 1
    za_ref[...] = logits(t)
    zb_ref[...] = logits(t - 1)
    step(za_ref[...], t, col + tk < row)
    za_ref[...] = logits(jnp.maximum(t - 2, 0))
    step(zb_ref[...], t - 1, col < row)

    def body(p, carry):
        t = 2 * (i - p) + 1
        zb_ref[...] = logits(t - 1)
        step(za_ref[...], t, None)
        za_ref[...] = logits(jnp.maximum(t - 2, 0))
        step(zb_ref[...], t - 1, None)
        return carry

    lax.fori_loop(1, i + 1, body, 0)
    o_ref[...] = acc_ref[...].astype(o_ref.dtype)


def _sb_attention(h, u, *, batch, seq, tq):
    nq = seq // tq
    nh = SB_HEADS
    kern = functools.partial(_sb_kernel, tq=tq)
    kcol = SB_WIDTH // HEAD_DIM
    return pl.pallas_call(
        kern,
        grid=(batch, nh, nq),
        in_specs=[pl.BlockSpec((tq, HEAD_DIM), lambda b, hh, i: (b * nq + i, hh)),
                  pl.BlockSpec((seq, HEAD_DIM), lambda b, hh, i: (b, kcol + hh)),
                  pl.BlockSpec((seq, HEAD_DIM), lambda b, hh, i: (b, 2 * kcol + hh)),
                  pl.BlockSpec((tq // 2, tq // 2), lambda b, hh, i: (0, 0))],
        out_specs=pl.BlockSpec((tq, HEAD_DIM), lambda b, hh, i: (b * nq + i, hh)),
        out_shape=jax.ShapeDtypeStruct((batch * seq, SB_WIDTH), BF16),
        scratch_shapes=[pltpu.VMEM((tq, HEAD_DIM), F32),
                        pltpu.VMEM((tq, LANES), F32),
                        pltpu.VMEM((tq, tq // 2), F32),
                        pltpu.VMEM((tq, tq // 2), F32)],
        compiler_params=_params(("parallel", "parallel", "arbitrary")),
        name="sb_attention",
    )(h, h, h, u)


def _softmax_step(s, vs, m_ref, l_ref, acc_ref):
    m_old = m_ref[...]
    m_new = jnp.maximum(m_old, jnp.max(s, axis=-1, keepdims=True))
    alpha = jnp.exp(m_old - m_new)
    ps = [jnp.exp(s[:, c * LANES:(c + 1) * LANES] - m_new) for c in range(s.shape[1] // LANES)]
    l_ref[...] = alpha * l_ref[...] + functools.reduce(lambda a, b: a + b, ps)
    pv = jnp.dot(jnp.concatenate(ps, axis=-1).astype(BF16), vs, preferred_element_type=F32)
    for c in range(pv.shape[1] // LANES):
        sl = slice(c * LANES, (c + 1) * LANES)
        acc_ref[:, sl] = alpha * acc_ref[:, sl] + pv[:, sl]
    m_ref[...] = m_new


def _causal_pair(tq, tk):
    row = lax.broadcasted_iota(jnp.int32, (tq, tk), 0)
    col = lax.broadcasted_iota(jnp.int32, (tq, tk), 1)
    return col <= row, col + tk <= row


def _diff_kernel(q_ref, k_ref, v_ref, lq1_ref, lk1_ref, lq2_ref, lk2_ref, g_ref, o_ref,
                 acc_ref, m_ref, l_ref, sa_ref, sb_ref, *, tq, lambda_init):
    tk = tq // 2
    i = pl.program_id(2)
    q = q_ref[...]
    m_ref[...] = jnp.full_like(m_ref, NEG_BIG)
    l_ref[...] = jnp.zeros_like(l_ref)
    acc_ref[...] = jnp.zeros_like(acc_ref)

    def scores(t, s_ref):
        start = pl.multiple_of(t * tk, tk)
        ks = k_ref[pl.ds(start, tk), :]
        for mp in range(2):
            sl = slice(mp * HEAD_DIM, (mp + 1) * HEAD_DIM)
            s_ref[mp] = lax.dot_general(q[:, sl], ks[:, sl], _NT, preferred_element_type=F32)

    def step(s_ref, t, mask):
        start = pl.multiple_of(t * tk, tk)
        vs = v_ref[pl.ds(start, tk), :]
        for mp in range(2):
            s = s_ref[mp]
            if mask is not None:
                s = jnp.where(mask, s, NEG_BIG)
            _softmax_step(s, vs, m_ref.at[mp], l_ref.at[mp], acc_ref.at[mp])

    scores(0, sa_ref)

    def body(p, c):
        t = 2 * p
        scores(t + 1, sb_ref)
        step(sa_ref, t, None)
        scores(t + 2, sa_ref)
        step(sb_ref, t + 1, None)
        return c

    lax.fori_loop(0, i, body, 0)
    t = 2 * i
    scores(t + 1, sb_ref)
    mask_a, mask_b = _causal_pair(tq, tk)
    step(sa_ref, t, mask_a)
    step(sb_ref, t + 1, mask_b)

    lam = (jnp.exp(jnp.sum(lq1_ref[...] * lk1_ref[...], axis=-1, keepdims=True))
           - jnp.exp(jnp.sum(lq2_ref[...] * lk2_ref[...], axis=-1, keepdims=True))
           + lambda_init)
    l0 = jnp.sum(l_ref[0], axis=-1, keepdims=True)
    l1 = jnp.sum(l_ref[1], axis=-1, keepdims=True)
    o = acc_ref[0] / l0 - lam * (acc_ref[1] / l1)
    o = _rms(o, g_ref[...], LN_EPS) * (1.0 - lambda_init)
    o_ref[...] = o.astype(o_ref.dtype)


def _diff_attention(h, lq1, lk1, lq2, lk2, g, *, batch, seq, tq, lambda_init):
    nq = seq // tq
    kern = functools.partial(_diff_kernel, tq=tq, lambda_init=lambda_init)
    qcol = 3 * SB_WIDTH // DIFF_V_DIM
    kcol = qcol + DIFF_QK_WIDTH // DIFF_V_DIM
    vcol = kcol + DIFF_QK_WIDTH // DIFF_V_DIM
    vec = pl.BlockSpec((1, HEAD_DIM), lambda b, hh, i: (0, 0))
    return pl.pallas_call(
        kern,
        grid=(batch, DIFF_HEADS, nq),
        in_specs=[pl.BlockSpec((tq, DIFF_V_DIM), lambda b, hh, i: (b * nq + i, qcol + hh)),
                  pl.BlockSpec((seq, DIFF_V_DIM), lambda b, hh, i: (b, kcol + hh)),
                  pl.BlockSpec((seq, DIFF_V_DIM), lambda b, hh, i: (b, vcol + hh)),
                  vec, vec, vec, vec,
                  pl.BlockSpec((1, DIFF_V_DIM), lambda b, hh, i: (0, 0))],
        out_specs=pl.BlockSpec((tq, DIFF_V_DIM), lambda b, hh, i: (b * nq + i, hh)),
        out_shape=jax.ShapeDtypeStruct((batch * seq, DIFF_V_WIDTH), BF16),
        scratch_shapes=[pltpu.VMEM((2, tq, DIFF_V_DIM), F32),
                        pltpu.VMEM((2, tq, LANES), F32),
                        pltpu.VMEM((2, tq, LANES), F32),
                        pltpu.VMEM((2, tq, tq // 2), F32),
                        pltpu.VMEM((2, tq, tq // 2), F32)],
        compiler_params=_params(("parallel", "parallel", "arbitrary")),
        name="diff_attention",
    )(h, h, h, lq1, lk1, lq2, lk2, g)


def _mla_kernel(q_ref, kn_ref, kpe_ref, v_ref, o_ref, acc_ref, m_ref, l_ref, sa_ref, sb_ref, *, tq):
    tk = tq // 2
    i = pl.program_id(2)
    q = q_ref[...]
    m_ref[...] = jnp.full_like(m_ref, NEG_BIG)
    l_ref[...] = jnp.zeros_like(l_ref)
    acc_ref[...] = jnp.zeros_like(acc_ref)

    def scores(t):
        start = pl.multiple_of(t * tk, tk)
        ks = jnp.concatenate([kn_ref[pl.ds(start, tk), :], kpe_ref[pl.ds(start, tk), :]], axis=-1)
        return lax.dot_general(q, ks, _NT, preferred_element_type=F32)

    def step(s, t):
        start = pl.multiple_of(t * tk, tk)
        _softmax_step(s, v_ref[pl.ds(start, tk), :], m_ref, l_ref, acc_ref)

    sa_ref[...] = scores(0)

    def body(p, c):
        t = 2 * p
        sb_ref[...] = scores(t + 1)
        step(sa_ref[...], t)
        sa_ref[...] = scores(t + 2)
        step(sb_ref[...], t + 1)
        return c

    lax.fori_loop(0, i, body, 0)
    t = 2 * i
    sb_ref[...] = scores(t + 1)
    mask_a, mask_b = _causal_pair(tq, tk)
    step(jnp.where(mask_a, sa_ref[...], NEG_BIG), t)
    step(jnp.where(mask_b, sb_ref[...], NEG_BIG), t + 1)
    l = jnp.sum(l_ref[...], axis=-1, keepdims=True)
    o_ref[...] = (acc_ref[...] / l).astype(o_ref.dtype)


def _mla_attention(q, kv, kpe, *, batch, seq, tq):
    nq = seq // tq
    tk = tq // 2
    kern = functools.partial(_mla_kernel, tq=tq)
    return pl.pallas_call(
        kern,
        grid=(batch, MLA_HEADS, nq),
        in_specs=[pl.BlockSpec((tq, 2 * LANES), lambda b, hh, i: (b * nq + i, hh)),
                  pl.BlockSpec((seq, LANES), lambda b, hh, i: (b, 2 * hh)),
                  pl.BlockSpec((seq, LANES), lambda b, hh, i: (b, 0)),
                  pl.BlockSpec((seq, LANES), lambda b, hh, i: (b, 2 * hh + 1))],
        out_specs=pl.BlockSpec((tq, MLA_V_DIM), lambda b, hh, i: (b * nq + i, hh)),
        out_shape=jax.ShapeDtypeStruct((batch * seq, MLA_HEADS * MLA_V_DIM), BF16),
        scratch_shapes=[pltpu.VMEM((tq, MLA_V_DIM), F32),
                        pltpu.VMEM((tq, LANES), F32),
                        pltpu.VMEM((tq, LANES), F32),
                        pltpu.VMEM((tq, tk), F32),
                        pltpu.VMEM((tq, tk), F32)],
        compiler_params=_params(("parallel", "parallel", "arbitrary")),
        name="mla_attention",
    )(q, kv, kpe, kv)


def _spread_rope_cols(w):
    half = MLA_ROPE_DIM // 2
    z = jnp.zeros(w.shape[:-1] + (LANES // 2 - half,), w.dtype)
    return jnp.concatenate([w[..., :half], z, w[..., half:], z], axis=-1)


def _rope_freqs():
    f_full = ROPE_THETA ** (-jnp.arange(0, HEAD_DIM, 2, dtype=F32) / HEAD_DIM)
    f_mla = ROPE_THETA ** (-jnp.arange(0, MLA_ROPE_DIM, 2, dtype=F32) / MLA_ROPE_DIM)
    row_full = jnp.concatenate([f_full, f_full])
    row_mla = _spread_rope_cols(jnp.concatenate([f_mla, f_mla]))
    sign = jnp.concatenate([-jnp.ones((1, LANES // 2), F32), jnp.ones((1, LANES // 2), F32)], axis=-1)
    return jnp.stack([row_full, row_mla]), sign


def kernel(x, positions, sb_diff_w_in, sb_diff_w_out, diff_lambda_q1, diff_lambda_k1,
           diff_lambda_q2, diff_lambda_k2, diff_subln_g, mla_w_in, mla_q_norm_g, mla_kv_norm_g,
           mla_w_q_up, mla_w_kv_up, mla_w_out, ffn_w_gate, ffn_w_up, ffn_w_down, ln_g, ln_b):
    batch, seq, d = x.shape
    t = batch * seq
    tq = min(seq, 512)
    sub = tq // 2
    bm = min(t, 1024)
    bm_ln = min(t, 512)

    freqs, sign = _rope_freqs()
    cos_tab, sin_tab = _rope_tables(positions.reshape(t, 1), freqs, sign)
    u = (lax.broadcasted_iota(jnp.int32, (sub, sub), 0)
         > lax.broadcasted_iota(jnp.int32, (sub, sub), 1)).astype(BF16)

    xf = x.reshape(t, d)
    xb = xf.astype(BF16)

    def ffn(layer, xf, xb, want_bf16):
        h = _ffn_up(xb, ffn_w_gate[layer].astype(BF16), ffn_w_up[layer].astype(BF16), bm=bm, bn=512)
        return _mm_res_ln(h, ffn_w_down[layer].astype(BF16), xf, ln_g[layer, 1][None], ln_b[layer, 1][None],
                          bm=bm_ln, bk=FFN_DIM // 4, want_bf16=want_bf16, name=f"ffn_down{layer}")

    h0 = _in_proj0(xb, sb_diff_w_in[0].astype(BF16), cos_tab, sin_tab, bm=bm, bn=512)
    oa = _sb_attention(h0, u, batch=batch, seq=seq, tq=tq)
    lambda_init = 0.8 - 0.6 * math.exp(-0.3 * 0)
    od = _diff_attention(h0, diff_lambda_q1[0][None], diff_lambda_k1[0][None], diff_lambda_q2[0][None],
                         diff_lambda_k2[0][None], diff_subln_g[0][None],
                         batch=batch, seq=seq, tq=tq, lambda_init=lambda_init)
    mix = jnp.concatenate([oa, od], axis=-1)
    xf, xb = _mm_res_ln(mix, sb_diff_w_out[0].astype(BF16), xf, ln_g[0, 0][None], ln_b[0, 0][None],
                        bm=bm_ln, bk=d, want_bf16=True, name="attn_out0")
    xf, xb = ffn(0, xf, xb, True)

    w_in1 = jnp.concatenate([mla_w_in[0][:, :MLA_Q_RANK + MLA_KV_RANK],
                             _spread_rope_cols(mla_w_in[0][:, MLA_Q_RANK + MLA_KV_RANK:])],
                            axis=-1).astype(BF16)
    wq = mla_w_q_up[0].reshape(MLA_Q_RANK, MLA_HEADS, MLA_QK_DIM)
    wq = jnp.concatenate([wq[..., :MLA_NOPE_DIM], _spread_rope_cols(wq[..., MLA_NOPE_DIM:])], axis=-1)
    wq = wq.reshape(MLA_Q_RANK, MLA_HEADS * 2 * LANES).astype(BF16)
    cq, ckv, kpe = _mla_in(xb, w_in1, mla_q_norm_g[0][None], mla_kv_norm_g[0][None], cos_tab, sin_tab,
                           bm=bm_ln)
    q1 = _mla_q_up(cq, wq, cos_tab, sin_tab, bm=bm, bn=1024)
    kv1 = _mm(ckv, mla_w_kv_up[0].astype(BF16), bm=bm, bn=1024, name="mla_kv_up")
    o1 = _mla_attention(q1, kv1, kpe, batch=batch, seq=seq, tq=tq)
    xf, xb = _mm_res_ln(o1, mla_w_out[0].astype(BF16), xf, ln_g[1, 0][None], ln_b[1, 0][None],
                        bm=bm_ln, bk=d, want_bf16=True, name="attn_out1")
    xf, _ = ffn(1, xf, xb, False)
    return xf.reshape(batch, seq, d)
```

```python
import functools
import math

import jax
import jax.numpy as jnp
from jax import lax
from jax.experimental import pallas as pl
from jax.experimental.pallas import tpu as pltpu

F32 = jnp.float32
BF16 = jnp.bfloat16

D_MODEL = 2048
DEPTH = 2
HEAD_DIM = 128
SB_HEADS = 8
DIFF_HEADS = 4
DIFF_V_DIM = 256
SB_WIDTH = SB_HEADS * HEAD_DIM
DIFF_QK_WIDTH = DIFF_HEADS * 2 * HEAD_DIM
DIFF_V_WIDTH = DIFF_HEADS * DIFF_V_DIM
EVEN_IN_WIDTH = 3 * SB_WIDTH + 2 * DIFF_QK_WIDTH + DIFF_V_WIDTH
MLA_HEADS = 16
MLA_Q_RANK = 512
MLA_KV_RANK = 512
MLA_NOPE_DIM = 128
MLA_ROPE_DIM = 64
MLA_V_DIM = 128
MLA_QK_DIM = MLA_NOPE_DIM + MLA_ROPE_DIM
FFN_DIM = 5632
ROPE_THETA = 10000.0
LN_EPS = 1e-5
RMS_EPS = 1e-6
DN_ALPHA = (2 * DEPTH) ** 0.25

LANES = 128
V7X_VMEM_LIMIT = 52 * 1024 * 1024
NEG_BIG = -1e30
LOG2E = math.log2(math.e)
SB_ZERO_LOG = -110.0

_NT = (((1,), (1,)), ((), ()))


def _params(sem):
    return pltpu.CompilerParams(dimension_semantics=sem, vmem_limit_bytes=V7X_VMEM_LIMIT)


def _rope(x, cos, sin_signed):
    return x * cos + pltpu.roll(x, 64, 1) * sin_signed


def _rope_tables_kernel(pos_ref, freq_ref, sign_ref, cos_ref, sin_ref):
    ang = pos_ref[...].astype(F32) * freq_ref[0]
    cos_ref[0] = jnp.cos(ang)
    sin_ref[0] = jnp.sin(ang) * sign_ref[...]


def _rope_tables(pos_col, freqs, sign):
    t = pos_col.shape[0]
    bt = min(t, 1024)
    nf = freqs.shape[0]
    return pl.pallas_call(
        _rope_tables_kernel,
        grid=(nf, t // bt),
        in_specs=[pl.BlockSpec((bt, 1), lambda f, i: (i, 0)),
                  pl.BlockSpec((1, 1, LANES), lambda f, i: (f, 0, 0)),
                  pl.BlockSpec((1, LANES), lambda f, i: (0, 0))],
        out_specs=[pl.BlockSpec((1, bt, LANES), lambda f, i: (f, i, 0)),
                   pl.BlockSpec((1, bt, LANES), lambda f, i: (f, i, 0))],
        out_shape=[jax.ShapeDtypeStruct((nf, t, LANES), F32)] * 2,
        compiler_params=_params(("arbitrary", "arbitrary")),
        name="rope_tables",
    )(pos_col, freqs.reshape(nf, 1, LANES), sign)


def _in_proj0_kernel(x_ref, w_ref, cos_ref, sin_ref, o_ref, wb_ref, *, bn, scale):
    @pl.when(pl.program_id(1) == 0)
    def _():
        wb_ref[...] = w_ref[...].astype(BF16)

    col = pl.program_id(0) * bn
    is_rope = (col >= 3 * SB_WIDTH) & (col < 3 * SB_WIDTH + 2 * DIFF_QK_WIDTH)
    is_sb_q = col < SB_WIDTH
    is_diff_q = (col >= 3 * SB_WIDTH) & (col < 3 * SB_WIDTH + DIFF_QK_WIDTH)
    rf = jnp.where(is_rope, 1.0, 0.0).astype(F32)
    sc = jnp.where(is_sb_q, scale, jnp.where(is_diff_q, scale * LOG2E, 1.0)).astype(F32)
    a = (cos_ref[0] * rf + (1.0 - rf)) * sc
    b = sin_ref[0] * (rf * sc)
    acc = jnp.dot(x_ref[...], wb_ref[...], preferred_element_type=F32)
    for g in range(bn // LANES):
        blk = acc[:, g * LANES:(g + 1) * LANES]
        o_ref[:, g * LANES:(g + 1) * LANES] = _rope(blk, a, b).astype(o_ref.dtype)


def _in_proj0(x, w, cos_tab, sin_tab, *, bm, bn):
    t, k = x.shape
    n = w.shape[1]
    kern = functools.partial(_in_proj0_kernel, bn=bn, scale=HEAD_DIM ** -0.5)
    return pl.pallas_call(
        kern,
        grid=(n // bn, t // bm),
        in_specs=[pl.BlockSpec((bm, k), lambda j, i: (i, 0)),
                  pl.BlockSpec((k, bn), lambda j, i: (0, j)),
                  pl.BlockSpec((1, bm, LANES), lambda j, i: (0, i, 0)),
                  pl.BlockSpec((1, bm, LANES), lambda j, i: (0, i, 0))],
        out_specs=pl.BlockSpec((bm, bn), lambda j, i: (i, j)),
        out_shape=jax.ShapeDtypeStruct((t, n), BF16),
        scratch_shapes=[pltpu.VMEM((k, bn), BF16)],
        compiler_params=_params(("arbitrary", "arbitrary")),
        name="in_proj0",
    )(x, w, cos_tab, sin_tab)


def _mm_kernel(x_ref, w_ref, o_ref):
    o_ref[...] = jnp.dot(x_ref[...], w_ref[...], preferred_element_type=F32).astype(o_ref.dtype)


def _mm(x, w, *, bm, bn, name):
    t, k = x.shape
    n = w.shape[1]
    return pl.pallas_call(
        _mm_kernel,
        grid=(t // bm, n // bn),
        in_specs=[pl.BlockSpec((bm, k), lambda i, j: (i, 0)),
                  pl.BlockSpec((k, bn), lambda i, j: (0, j))],
        out_specs=pl.BlockSpec((bm, bn), lambda i, j: (i, j)),
        out_shape=jax.ShapeDtypeStruct((t, n), BF16),
        compiler_params=_params(("parallel", "arbitrary")),
        name=name,
    )(x, w)


def _mla_q_up_kernel(x_ref, w_ref, cos_ref, sin_ref, o_ref, *, bn, scale):
    acc = jnp.dot(x_ref[...], w_ref[...], preferred_element_type=F32) * scale
    cos = cos_ref[0]
    sin = sin_ref[0]
    for g in range(bn // LANES):
        blk = acc[:, g * LANES:(g + 1) * LANES]
        if g % 2 == 1:
            blk = _rope(blk, cos, sin)
        o_ref[:, g * LANES:(g + 1) * LANES] = blk.astype(o_ref.dtype)


def _mla_q_up(x, w, cos_tab, sin_tab, *, bm, bn):
    t, k = x.shape
    n = w.shape[1]
    kern = functools.partial(_mla_q_up_kernel, bn=bn, scale=MLA_QK_DIM ** -0.5 * LOG2E)
    return pl.pallas_call(
        kern,
        grid=(t // bm, n // bn),
        in_specs=[pl.BlockSpec((bm, k), lambda i, j: (i, 0)),
                  pl.BlockSpec((k, bn), lambda i, j: (0, j)),
                  pl.BlockSpec((1, bm, LANES), lambda i, j: (1, i, 0)),
                  pl.BlockSpec((1, bm, LANES), lambda i, j: (1, i, 0))],
        out_specs=pl.BlockSpec((bm, bn), lambda i, j: (i, j)),
        out_shape=jax.ShapeDtypeStruct((t, n), BF16),
        compiler_params=_params(("parallel", "arbitrary")),
        name="mla_q_up",
    )(x, w, cos_tab, sin_tab)


def _rms(x, g, eps):
    return x * lax.rsqrt(jnp.mean(x * x, axis=-1, keepdims=True) + eps) * g


def _mla_in_kernel(x_ref, w_ref, gq_ref, gkv_ref, cos_ref, sin_ref, cq_ref, ckv_ref, kpe_ref):
    acc = jnp.dot(x_ref[...], w_ref[...], preferred_element_type=F32)
    cq_ref[...] = _rms(acc[:, :MLA_Q_RANK], gq_ref[...], RMS_EPS).astype(cq_ref.dtype)
    ckv_ref[...] = _rms(acc[:, MLA_Q_RANK:MLA_Q_RANK + MLA_KV_RANK], gkv_ref[...],
                        RMS_EPS).astype(ckv_ref.dtype)
    kpe = acc[:, MLA_Q_RANK + MLA_KV_RANK:]
    kpe_ref[...] = _rope(kpe, cos_ref[0], sin_ref[0]).astype(kpe_ref.dtype)


def _mla_in(x, w, gq, gkv, cos_tab, sin_tab, *, bm):
    t, k = x.shape
    n = w.shape[1]
    row = lambda i: (i, 0)
    fixed = lambda i: (0, 0)
    return pl.pallas_call(
        _mla_in_kernel,
        grid=(t // bm,),
        in_specs=[pl.BlockSpec((bm, k), row),
                  pl.BlockSpec((k, n), fixed),
                  pl.BlockSpec((1, MLA_Q_RANK), fixed),
                  pl.BlockSpec((1, MLA_KV_RANK), fixed),
                  pl.BlockSpec((1, bm, LANES), lambda i: (1, i, 0)),
                  pl.BlockSpec((1, bm, LANES), lambda i: (1, i, 0))],
        out_specs=[pl.BlockSpec((bm, MLA_Q_RANK), row),
                   pl.BlockSpec((bm, MLA_KV_RANK), row),
                   pl.BlockSpec((bm, LANES), row)],
        out_shape=[jax.ShapeDtypeStruct((t, MLA_Q_RANK), BF16),
                   jax.ShapeDtypeStruct((t, MLA_KV_RANK), BF16),
                   jax.ShapeDtypeStruct((t, LANES), BF16)],
        compiler_params=_params(("parallel",)),
        name="mla_in",
    )(x, w, gq, gkv, cos_tab, sin_tab)


def _mm_res_ln_kernel(a_ref, w_ref, res_ref, g_ref, b_ref, *rest, nk, want_bf16):
    if want_bf16:
        of_ref, ob_ref = rest[0], rest[1]
        scratch = rest[2:]
    else:
        of_ref, ob_ref = rest[0], None
        scratch = rest[1:]
    part = jnp.dot(a_ref[...], w_ref[...], preferred_element_type=F32)

    def finish(acc):
        y = DN_ALPHA * res_ref[...] + acc
        mu = jnp.mean(y, axis=-1, keepdims=True)
        d = y - mu
        var = jnp.mean(d * d, axis=-1, keepdims=True)
        out = d * lax.rsqrt(var + LN_EPS) * g_ref[...] + b_ref[...]
        of_ref[...] = out
        if ob_ref is not None:
            ob_ref[...] = out.astype(ob_ref.dtype)

    if nk == 1:
        finish(part)
        return
    acc_ref = scratch[0]
    k = pl.program_id(1)

    @pl.when(k == 0)
    def _():
        acc_ref[...] = part

    @pl.when((k > 0) & (k < nk - 1))
    def _():
        acc_ref[...] += part

    @pl.when(k == nk - 1)
    def _():
        finish(acc_ref[...] + part)


def _mm_res_ln(a, w, res, g, b, *, bm, bk, want_bf16, name):
    t, kdim = a.shape
    n = w.shape[1]
    nk = kdim // bk
    kern = functools.partial(_mm_res_ln_kernel, nk=nk, want_bf16=want_bf16)
    row = lambda i, k: (i, 0)
    out_specs = [pl.BlockSpec((bm, n), row)]
    out_shape = [jax.ShapeDtypeStruct((t, n), F32)]
    if want_bf16:
        out_specs.append(pl.BlockSpec((bm, n), row))
        out_shape.append(jax.ShapeDtypeStruct((t, n), BF16))
    outs = pl.pallas_call(
        kern,
        grid=(t // bm, nk),
        in_specs=[pl.BlockSpec((bm, bk), lambda i, k: (i, k)),
                  pl.BlockSpec((bk, n), lambda i, k: (k, 0)),
                  pl.BlockSpec((bm, n), row),
                  pl.BlockSpec((1, n), lambda i, k: (0, 0)),
                  pl.BlockSpec((1, n), lambda i, k: (0, 0))],
        out_specs=out_specs,
        out_shape=out_shape,
        scratch_shapes=[pltpu.VMEM((bm, n), F32)] if nk > 1 else [],
        compiler_params=_params(("parallel", "arbitrary")),
        name=name,
    )(a, w, res, g, b)
    return outs if want_bf16 else (outs[0], None)


def _ffn_up_kernel(x_ref, wg_ref, wu_ref, o_ref, wgb_ref, wub_ref):
    @pl.when(pl.program_id(1) == 0)
    def _():
        wgb_ref[...] = wg_ref[...].astype(BF16)
        wub_ref[...] = wu_ref[...].astype(BF16)

    x = x_ref[...]
    gate = jnp.dot(x, wgb_ref[...], preferred_element_type=F32)
    up = jnp.dot(x, wub_ref[...], preferred_element_type=F32)
    o_ref[...] = (gate * jax.nn.sigmoid(gate) * up).astype(o_ref.dtype)


def _ffn_up(x, wg, wu, *, bm, bn):
    t, k = x.shape
    n = wg.shape[1]
    return pl.pallas_call(
        _ffn_up_kernel,
        grid=(n // bn, t // bm),
        in_specs=[pl.BlockSpec((bm, k), lambda j, i: (i, 0)),
                  pl.BlockSpec((k, bn), lambda j, i: (0, j)),
                  pl.BlockSpec((k, bn), lambda j, i: (0, j))],
        out_specs=pl.BlockSpec((bm, bn), lambda j, i: (i, j)),
        out_shape=jax.ShapeDtypeStruct((t, n), BF16),
        scratch_shapes=[pltpu.VMEM((k, bn), BF16), pltpu.VMEM((k, bn), BF16)],
        compiler_params=_params(("arbitrary", "arbitrary")),
        name="ffn_up",
    )(x, wg, wu)


def _sb_kernel(q_ref, k_ref, v_ref, u_ref, o_ref, acc_ref, c_ref, za_ref, zb_ref, *, tq):
    tk = tq // 2
    i = pl.program_id(2)
    q = q_ref[...]
    u = u_ref[...]
    acc_ref[...] = jnp.zeros_like(acc_ref)
    c_ref[...] = jnp.zeros_like(c_ref)

    def logits(t):
        start = pl.multiple_of(t * tk, tk)
        return lax.dot_general(q, k_ref[pl.ds(start, tk), :], _NT, preferred_element_type=F32)

    def step(z, t, mask):
        start = pl.multiple_of(t * tk, tk)
        log_beta = jnp.minimum(z, 0.0) - jnp.log(1.0 + jnp.exp(-jnp.abs(z)))
        log_rest = log_beta - z
        if mask is not None:
            log_rest = jnp.where(mask, log_rest, 0.0)
        hi = log_rest.astype(BF16)
        lo = (log_rest - hi.astype(F32)).astype(BF16)
        later = jnp.dot(hi, u, preferred_element_type=F32) + jnp.dot(lo, u, preferred_element_type=F32)
        carry = c_ref[...]
        ws = []
        for c in range(tk // LANES):
            sl = slice(c * LANES, (c + 1) * LANES)
            w = jnp.exp(log_beta[:, sl] + later[:, sl] + carry)
            if mask is not None:
                w = jnp.where(mask[:, sl], w, 0.0)
            ws.append(w.astype(BF16))
        acc_ref[...] += jnp.dot(jnp.concatenate(ws, axis=-1), v_ref[pl.ds(start, tk), :],
                                preferred_element_type=F32)
        carry = carry + jnp.sum(log_rest, axis=-1, keepdims=True)
        c_ref[...] = carry
        return carry

    row = lax.broadcasted_iota(jnp.int32, (tq, tk), 0)
    col = lax.broadcasted_iota(jnp.int32, (tq, tk), 1)
    t = 2 * i + 1
    za_ref[...] = logits(t)
    zb_ref[...] = logits(t - 1)
    step(za_ref[...], t, col + tk < row)
    za_ref[...] = logits(jnp.maximum(t - 2, 0))
    carry = step(zb_ref[...], t - 1, col < row)

    def alive(state):
        p, top = state
        return (p <= i) & (top > SB_ZERO_LOG)

    def body(state):
        p, _ = state
        t = 2 * (i - p) + 1
        zb_ref[...] = logits(t - 1)
        step(za_ref[...], t, None)
        za_ref[...] = logits(jnp.maximum(t - 2, 0))
        carry = step(zb_ref[...], t - 1, None)
        return p + 1, jnp.max(carry)

    lax.while_loop(alive, body, (jnp.int32(1), jnp.max(carry)))
    o_ref[...] = acc_ref[...].astype(o_ref.dtype)


def _sb_attention(h, u, *, batch, seq, tq):
    nq = seq // tq
    nh = SB_HEADS
    kern = functools.partial(_sb_kernel, tq=tq)
    kcol = SB_WIDTH // HEAD_DIM
    return pl.pallas_call(
        kern,
        grid=(batch, nh, nq),
        in_specs=[pl.BlockSpec((tq, HEAD_DIM), lambda b, hh, i: (b * nq + i, hh)),
                  pl.BlockSpec((seq, HEAD_DIM), lambda b, hh, i: (b, kcol + hh)),
                  pl.BlockSpec((seq, HEAD_DIM), lambda b, hh, i: (b, 2 * kcol + hh)),
                  pl.BlockSpec((tq // 2, tq // 2), lambda b, hh, i: (0, 0))],
        out_specs=pl.BlockSpec((tq, HEAD_DIM), lambda b, hh, i: (b * nq + i, hh)),
        out_shape=jax.ShapeDtypeStruct((batch * seq, SB_WIDTH), BF16),
        scratch_shapes=[pltpu.VMEM((tq, HEAD_DIM), F32),
                        pltpu.VMEM((tq, LANES), F32),
                        pltpu.VMEM((tq, tq // 2), F32),
                        pltpu.VMEM((tq, tq // 2), F32)],
        compiler_params=_params(("parallel", "parallel", "arbitrary")),
        name="sb_attention",
    )(h, h, h, u)


def _softmax_step(s, vs, m_ref, l_ref, acc_ref):
    m_old = m_ref[...]
    m_new = jnp.maximum(m_old, jnp.max(s, axis=-1, keepdims=True))
    alpha = jnp.exp2(m_old - m_new)
    ps = [jnp.exp2(s[:, c * LANES:(c + 1) * LANES] - m_new) for c in range(s.shape[1] // LANES)]
    l_ref[...] = alpha * l_ref[...] + functools.reduce(lambda a, b: a + b, ps)
    pv = jnp.dot(jnp.concatenate(ps, axis=-1).astype(BF16), vs, preferred_element_type=F32)
    for c in range(pv.shape[1] // LANES):
        sl = slice(c * LANES, (c + 1) * LANES)
        acc_ref[:, sl] = alpha * acc_ref[:, sl] + pv[:, sl]
    m_ref[...] = m_new


def _causal_pair(tq, tk):
    row = lax.broadcasted_iota(jnp.int32, (tq, tk), 0)
    col = lax.broadcasted_iota(jnp.int32, (tq, tk), 1)
    return col <= row, col + tk <= row


def _diff_kernel(q_ref, k_ref, v_ref, lq1_ref, lk1_ref, lq2_ref, lk2_ref, g_ref, o_ref,
                 acc_ref, m_ref, l_ref, sa_ref, sb_ref, *, tq, lambda_init):
    tk = tq // 2
    i = pl.program_id(2)
    q = q_ref[...]
    m_ref[...] = jnp.full_like(m_ref, NEG_BIG)
    l_ref[...] = jnp.zeros_like(l_ref)
    acc_ref[...] = jnp.zeros_like(acc_ref)

    def scores(t, s_ref):
        start = pl.multiple_of(t * tk, tk)
        ks = k_ref[pl.ds(start, tk), :]
        for mp in range(2):
            sl = slice(mp * HEAD_DIM, (mp + 1) * HEAD_DIM)
            s_ref[mp] = lax.dot_general(q[:, sl], ks[:, sl], _NT, preferred_element_type=F32)

    def step(s_ref, t, mask):
        start = pl.multiple_of(t * tk, tk)
        vs = v_ref[pl.ds(start, tk), :]
        for mp in range(2):
            s = s_ref[mp]
            if mask is not None:
                s = jnp.where(mask, s, NEG_BIG)
            _softmax_step(s, vs, m_ref.at[mp], l_ref.at[mp], acc_ref.at[mp])

    scores(0, sa_ref)

    def body(p, c):
        t = 2 * p
        scores(t + 1, sb_ref)
        step(sa_ref, t, None)
        scores(t + 2, sa_ref)
        step(sb_ref, t + 1, None)
        return c

    lax.fori_loop(0, i, body, 0)
    t = 2 * i
    scores(t + 1, sb_ref)
    mask_a, mask_b = _causal_pair(tq, tk)
    step(sa_ref, t, mask_a)
    step(sb_ref, t + 1, mask_b)

    lam = (jnp.exp(jnp.sum(lq1_ref[...] * lk1_ref[...], axis=-1, keepdims=True))
           - jnp.exp(jnp.sum(lq2_ref[...] * lk2_ref[...], axis=-1, keepdims=True))
           + lambda_init)
    l0 = jnp.sum(l_ref[0], axis=-1, keepdims=True)
    l1 = jnp.sum(l_ref[1], axis=-1, keepdims=True)
    o = acc_ref[0] / l0 - lam * (acc_ref[1] / l1)
    o = _rms(o, g_ref[...], LN_EPS) * (1.0 - lambda_init)
    o_ref[...] = o.astype(o_ref.dtype)


def _diff_attention(h, lq1, lk1, lq2, lk2, g, *, batch, seq, tq, lambda_init):
    nq = seq // tq
    kern = functools.partial(_diff_kernel, tq=tq, lambda_init=lambda_init)
    qcol = 3 * SB_WIDTH // DIFF_V_DIM
    kcol = qcol + DIFF_QK_WIDTH // DIFF_V_DIM
    vcol = kcol + DIFF_QK_WIDTH // DIFF_V_DIM
    vec = pl.BlockSpec((1, HEAD_DIM), lambda b, hh, i: (0, 0))
    return pl.pallas_call(
        kern,
        grid=(batch, DIFF_HEADS, nq),
        in_specs=[pl.BlockSpec((tq, DIFF_V_DIM), lambda b, hh, i: (b * nq + i, qcol + hh)),
                  pl.BlockSpec((seq, DIFF_V_DIM), lambda b, hh, i: (b, kcol + hh)),
                  pl.BlockSpec((seq, DIFF_V_DIM), lambda b, hh, i: (b, vcol + hh)),
                  vec, vec, vec, vec,
                  pl.BlockSpec((1, DIFF_V_DIM), lambda b, hh, i: (0, 0))],
        out_specs=pl.BlockSpec((tq, DIFF_V_DIM), lambda b, hh, i: (b * nq + i, hh)),
        out_shape=jax.ShapeDtypeStruct((batch * seq, DIFF_V_WIDTH), BF16),
        scratch_shapes=[pltpu.VMEM((2, tq, DIFF_V_DIM), F32),
                        pltpu.VMEM((2, tq, LANES), F32),
                        pltpu.VMEM((2, tq, LANES), F32),
                        pltpu.VMEM((2, tq, tq // 2), F32),
                        pltpu.VMEM((2, tq, tq // 2), F32)],
        compiler_params=_params(("parallel", "parallel", "arbitrary")),
        name="diff_attention",
    )(h, h, h, lq1, lk1, lq2, lk2, g)


def _mla_kernel(q_ref, kn_ref, kpe_ref, v_ref, o_ref, acc_ref, m_ref, l_ref, sa_ref, sb_ref, *, tq):
    tk = tq // 2
    i = pl.program_id(2)
    q = q_ref[...]
    m_ref[...] = jnp.full_like(m_ref, NEG_BIG)
    l_ref[...] = jnp.zeros_like(l_ref)
    acc_ref[...] = jnp.zeros_like(acc_ref)

    def scores(t):
        start = pl.multiple_of(t * tk, tk)
        ks = jnp.concatenate([kn_ref[pl.ds(start, tk), :], kpe_ref[pl.ds(start, tk), :]], axis=-1)
        return lax.dot_general(q, ks, _NT, preferred_element_type=F32)

    def step(s, t):
        start = pl.multiple_of(t * tk, tk)
        _softmax_step(s, v_ref[pl.ds(start, tk), :], m_ref, l_ref, acc_ref)

    sa_ref[...] = scores(0)

    def body(p, c):
        t = 2 * p
        sb_ref[...] = scores(t + 1)
        step(sa_ref[...], t)
        sa_ref[...] = scores(t + 2)
        step(sb_ref[...], t + 1)
        return c

    lax.fori_loop(0, i, body, 0)
    t = 2 * i
    sb_ref[...] = scores(t + 1)
    mask_a, mask_b = _causal_pair(tq, tk)
    step(jnp.where(mask_a, sa_ref[...], NEG_BIG), t)
    step(jnp.where(mask_b, sb_ref[...], NEG_BIG), t + 1)
    l = jnp.sum(l_ref[...], axis=-1, keepdims=True)
    o_ref[...] = (acc_ref[...] / l).astype(o_ref.dtype)


def _mla_attention(q, kv, kpe, *, batch, seq, tq):
    nq = seq // tq
    tk = tq // 2
    kern = functools.partial(_mla_kernel, tq=tq)
    return pl.pallas_call(
        kern,
        grid=(batch, MLA_HEADS, nq),
        in_specs=[pl.BlockSpec((tq, 2 * LANES), lambda b, hh, i: (b * nq + i, hh)),
                  pl.BlockSpec((seq, LANES), lambda b, hh, i: (b, 2 * hh)),
                  pl.BlockSpec((seq, LANES), lambda b, hh, i: (b, 0)),
                  pl.BlockSpec((seq, LANES), lambda b, hh, i: (b, 2 * hh + 1))],
        out_specs=pl.BlockSpec((tq, MLA_V_DIM), lambda b, hh, i: (b * nq + i, hh)),
        out_shape=jax.ShapeDtypeStruct((batch * seq, MLA_HEADS * MLA_V_DIM), BF16),
        scratch_shapes=[pltpu.VMEM((tq, MLA_V_DIM), F32),
                        pltpu.VMEM((tq, LANES), F32),
                        pltpu.VMEM((tq, LANES), F32),
                        pltpu.VMEM((tq, tk), F32),
                        pltpu.VMEM((tq, tk), F32)],
        compiler_params=_params(("parallel", "parallel", "arbitrary")),
        name="mla_attention",
    )(q, kv, kpe, kv)


def _spread_rope_cols(w):
    half = MLA_ROPE_DIM // 2
    z = jnp.zeros(w.shape[:-1] + (LANES // 2 - half,), w.dtype)
    return jnp.concatenate([w[..., :half], z, w[..., half:], z], axis=-1)


def _rope_freqs():
    f_full = ROPE_THETA ** (-jnp.arange(0, HEAD_DIM, 2, dtype=F32) / HEAD_DIM)
    f_mla = ROPE_THETA ** (-jnp.arange(0, MLA_ROPE_DIM, 2, dtype=F32) / MLA_ROPE_DIM)
    row_full = jnp.concatenate([f_full, f_full])
    row_mla = _spread_rope_cols(jnp.concatenate([f_mla, f_mla]))
    sign = jnp.concatenate([-jnp.ones((1, LANES // 2), F32), jnp.ones((1, LANES // 2), F32)], axis=-1)
    return jnp.stack([row_full, row_mla]), sign


def kernel(x, positions, sb_diff_w_in, sb_diff_w_out, diff_lambda_q1, diff_lambda_k1,
           diff_lambda_q2, diff_lambda_k2, diff_subln_g, mla_w_in, mla_q_norm_g, mla_kv_norm_g,
           mla_w_q_up, mla_w_kv_up, mla_w_out, ffn_w_gate, ffn_w_up, ffn_w_down, ln_g, ln_b):
    batch, seq, d = x.shape
    t = batch * seq
    tq = min(seq, 512)
    sub = tq // 2
    bm = min(t, 1024)
    bm_ln = min(t, 512)

    freqs, sign = _rope_freqs()
    cos_tab, sin_tab = _rope_tables(positions.reshape(t, 1), freqs, sign)
    u = (lax.broadcasted_iota(jnp.int32, (sub, sub), 0)
         > lax.broadcasted_iota(jnp.int32, (sub, sub), 1)).astype(BF16)

    xf = x.reshape(t, d)
    xb = xf.astype(BF16)

    def ffn(layer, xf, xb, want_bf16):
        h = _ffn_up(xb, ffn_w_gate[layer], ffn_w_up[layer], bm=bm, bn=512)
        return _mm_res_ln(h, ffn_w_down[layer].astype(BF16), xf, ln_g[layer, 1][None], ln_b[layer, 1][None],
                          bm=bm_ln, bk=FFN_DIM // 4, want_bf16=want_bf16, name=f"ffn_down{layer}")

    h0 = _in_proj0(xb, sb_diff_w_in[0], cos_tab, sin_tab, bm=bm, bn=512)
    oa = _sb_attention(h0, u, batch=batch, seq=seq, tq=tq)
    lambda_init = 0.8 - 0.6 * math.exp(-0.3 * 0)
    od = _diff_attention(h0, diff_lambda_q1[0][None], diff_lambda_k1[0][None], diff_lambda_q2[0][None],
                         diff_lambda_k2[0][None], diff_subln_g[0][None],
                         batch=batch, seq=seq, tq=tq, lambda_init=lambda_init)
    mix = jnp.concatenate([oa, od], axis=-1)
    xf, xb = _mm_res_ln(mix, sb_diff_w_out[0].astype(BF16), xf, ln_g[0, 0][None], ln_b[0, 0][None],
                        bm=bm_ln, bk=d, want_bf16=True, name="attn_out0")
    xf, xb = ffn(0, xf, xb, True)

    w_in1 = jnp.concatenate([mla_w_in[0][:, :MLA_Q_RANK + MLA_KV_RANK],
                             _spread_rope_cols(mla_w_in[0][:, MLA_Q_RANK + MLA_KV_RANK:])],
                            axis=-1).astype(BF16)
    wq = mla_w_q_up[0].reshape(MLA_Q_RANK, MLA_HEADS, MLA_QK_DIM)
    wq = jnp.concatenate([wq[..., :MLA_NOPE_DIM], _spread_rope_cols(wq[..., MLA_NOPE_DIM:])], axis=-1)
    wq = wq.reshape(MLA_Q_RANK, MLA_HEADS * 2 * LANES).astype(BF16)
    cq, ckv, kpe = _mla_in(xb, w_in1, mla_q_norm_g[0][None], mla_kv_norm_g[0][None], cos_tab, sin_tab,
                           bm=bm_ln)
    q1 = _mla_q_up(cq, wq, cos_tab, sin_tab, bm=bm, bn=1024)
    kv1 = _mm(ckv, mla_w_kv_up[0].astype(BF16), bm=bm, bn=1024, name="mla_kv_up")
    o1 = _mla_attention(q1, kv1, kpe, batch=batch, seq=seq, tq=tq)
    xf, xb = _mm_res_ln(o1, mla_w_out[0].astype(BF16), xf, ln_g[1, 0][None], ln_b[1, 0][None],
                        bm=bm_ln, bk=d, want_bf16=True, name="attn_out1")
    xf, _ = ffn(1, xf, xb, False)
    return xf.reshape(batch, seq, d)
```

```python
import functools
import math

import jax
import jax.numpy as jnp
from jax import lax
from jax.experimental import pallas as pl
from jax.experimental.pallas import tpu as pltpu

F32 = jnp.float32
BF16 = jnp.bfloat16

D_MODEL = 2048
DEPTH = 2
HEAD_DIM = 128
SB_HEADS = 8
DIFF_HEADS = 4
DIFF_V_DIM = 256
SB_WIDTH = SB_HEADS * HEAD_DIM
DIFF_QK_WIDTH = DIFF_HEADS * 2 * HEAD_DIM
DIFF_V_WIDTH = DIFF_HEADS * DIFF_V_DIM
EVEN_IN_WIDTH = 3 * SB_WIDTH + 2 * DIFF_QK_WIDTH + DIFF_V_WIDTH
MLA_HEADS = 16
MLA_Q_RANK = 512
MLA_KV_RANK = 512
MLA_NOPE_DIM = 128
MLA_ROPE_DIM = 64
MLA_V_DIM = 128
MLA_QK_DIM = MLA_NOPE_DIM + MLA_ROPE_DIM
FFN_DIM = 5632
ROPE_THETA = 10000.0
LN_EPS = 1e-5
RMS_EPS = 1e-6
DN_ALPHA = (2 * DEPTH) ** 0.25

LANES = 128
V7X_VMEM_LIMIT = 52 * 1024 * 1024
NEG_BIG = -1e30
LOG2E = math.log2(math.e)
SB_ZERO_LOG = -110.0

_NT = (((1,), (1,)), ((), ()))


def _params(sem):
    return pltpu.CompilerParams(dimension_semantics=sem, vmem_limit_bytes=V7X_VMEM_LIMIT)


def _rope(x, cos, sin_signed):
    return x * cos + pltpu.roll(x, 64, 1) * sin_signed


def _rope_tables_kernel(pos_ref, freq_ref, sign_ref, cos_ref, sin_ref):
    ang = pos_ref[...].astype(F32) * freq_ref[0]
    cos_ref[0] = jnp.cos(ang)
    sin_ref[0] = jnp.sin(ang) * sign_ref[...]


def _rope_tables(pos_col, freqs, sign):
    t = pos_col.shape[0]
    bt = min(t, 1024)
    nf = freqs.shape[0]
    return pl.pallas_call(
        _rope_tables_kernel,
        grid=(nf, t // bt),
        in_specs=[pl.BlockSpec((bt, 1), lambda f, i: (i, 0)),
                  pl.BlockSpec((1, 1, LANES), lambda f, i: (f, 0, 0)),
                  pl.BlockSpec((1, LANES), lambda f, i: (0, 0))],
        out_specs=[pl.BlockSpec((1, bt, LANES), lambda f, i: (f, i, 0)),
                   pl.BlockSpec((1, bt, LANES), lambda f, i: (f, i, 0))],
        out_shape=[jax.ShapeDtypeStruct((nf, t, LANES), F32)] * 2,
        compiler_params=_params(("arbitrary", "arbitrary")),
        name="rope_tables",
    )(pos_col, freqs.reshape(nf, 1, LANES), sign)


def _in_proj0_kernel(x_ref, w_ref, cos_ref, sin_ref, o_ref, wb_ref, *, bn, scale):
    @pl.when(pl.program_id(1) == 0)
    def _():
        wb_ref[...] = w_ref[...].astype(BF16)

    col = pl.program_id(0) * bn
    is_rope = (col >= 3 * SB_WIDTH) & (col < 3 * SB_WIDTH + 2 * DIFF_QK_WIDTH)
    is_sb_q = col < SB_WIDTH
    is_diff_q = (col >= 3 * SB_WIDTH) & (col < 3 * SB_WIDTH + DIFF_QK_WIDTH)
    rf = jnp.where(is_rope, 1.0, 0.0).astype(F32)
    sc = jnp.where(is_sb_q, scale, jnp.where(is_diff_q, scale * LOG2E, 1.0)).astype(F32)
    a = (cos_ref[0] * rf + (1.0 - rf)) * sc
    b = sin_ref[0] * (rf * sc)
    acc = jnp.dot(x_ref[...], wb_ref[...], preferred_element_type=F32)
    for g in range(bn // LANES):
        blk = acc[:, g * LANES:(g + 1) * LANES]
        o_ref[:, g * LANES:(g + 1) * LANES] = _rope(blk, a, b).astype(o_ref.dtype)


def _in_proj0(x, w, cos_tab, sin_tab, *, bm, bn):
    t, k = x.shape
    n = w.shape[1]
    kern = functools.partial(_in_proj0_kernel, bn=bn, scale=HEAD_DIM ** -0.5)
    return pl.pallas_call(
        kern,
        grid=(n // bn, t // bm),
        in_specs=[pl.BlockSpec((bm, k), lambda j, i: (i, 0)),
                  pl.BlockSpec((k, bn), lambda j, i: (0, j)),
                  pl.BlockSpec((1, bm, LANES), lambda j, i: (0, i, 0)),
                  pl.BlockSpec((1, bm, LANES), lambda j, i: (0, i, 0))],
        out_specs=pl.BlockSpec((bm, bn), lambda j, i: (i, j)),
        out_shape=jax.ShapeDtypeStruct((t, n), BF16),
        scratch_shapes=[pltpu.VMEM((k, bn), BF16)],
        compiler_params=_params(("arbitrary", "arbitrary")),
        name="in_proj0",
    )(x, w, cos_tab, sin_tab)


def _mm_kernel(x_ref, w_ref, o_ref):
    o_ref[...] = jnp.dot(x_ref[...], w_ref[...], preferred_element_type=F32).astype(o_ref.dtype)


def _mm(x, w, *, bm, bn, name):
    t, k = x.shape
    n = w.shape[1]
    return pl.pallas_call(
        _mm_kernel,
        grid=(t // bm, n // bn),
        in_specs=[pl.BlockSpec((bm, k), lambda i, j: (i, 0)),
                  pl.BlockSpec((k, bn), lambda i, j: (0, j))],
        out_specs=pl.BlockSpec((bm, bn), lambda i, j: (i, j)),
        out_shape=jax.ShapeDtypeStruct((t, n), BF16),
        compiler_params=_params(("parallel", "arbitrary")),
        name=name,
    )(x, w)


def _mla_q_up_kernel(x_ref, w_ref, cos_ref, sin_ref, o_ref, *, bn, scale):
    acc = jnp.dot(x_ref[...], w_ref[...], preferred_element_type=F32) * scale
    cos = cos_ref[0]
    sin = sin_ref[0]
    for g in range(bn // LANES):
        blk = acc[:, g * LANES:(g + 1) * LANES]
        if g % 2 == 1:
            blk = _rope(blk, cos, sin)
        o_ref[:, g * LANES:(g + 1) * LANES] = blk.astype(o_ref.dtype)


def _mla_q_up(x, w, cos_tab, sin_tab, *, bm, bn):
    t, k = x.shape
    n = w.shape[1]
    kern = functools.partial(_mla_q_up_kernel, bn=bn, scale=MLA_QK_DIM ** -0.5 * LOG2E)
    return pl.pallas_call(
        kern,
        grid=(t // bm, n // bn),
        in_specs=[pl.BlockSpec((bm, k), lambda i, j: (i, 0)),
                  pl.BlockSpec((k, bn), lambda i, j: (0, j)),
                  pl.BlockSpec((1, bm, LANES), lambda i, j: (1, i, 0)),
                  pl.BlockSpec((1, bm, LANES), lambda i, j: (1, i, 0))],
        out_specs=pl.BlockSpec((bm, bn), lambda i, j: (i, j)),
        out_shape=jax.ShapeDtypeStruct((t, n), BF16),
        compiler_params=_params(("parallel", "arbitrary")),
        name="mla_q_up",
    )(x, w, cos_tab, sin_tab)


def _rms(x, g, eps):
    return x * lax.rsqrt(jnp.mean(x * x, axis=-1, keepdims=True) + eps) * g


def _mla_in_kernel(x_ref, w_ref, gq_ref, gkv_ref, cos_ref, sin_ref, cq_ref, ckv_ref, kpe_ref):
    acc = jnp.dot(x_ref[...], w_ref[...], preferred_element_type=F32)
    cq_ref[...] = _rms(acc[:, :MLA_Q_RANK], gq_ref[...], RMS_EPS).astype(cq_ref.dtype)
    ckv_ref[...] = _rms(acc[:, MLA_Q_RANK:MLA_Q_RANK + MLA_KV_RANK], gkv_ref[...],
                        RMS_EPS).astype(ckv_ref.dtype)
    kpe = acc[:, MLA_Q_RANK + MLA_KV_RANK:]
    kpe_ref[...] = _rope(kpe, cos_ref[0], sin_ref[0]).astype(kpe_ref.dtype)


def _mla_in(x, w, gq, gkv, cos_tab, sin_tab, *, bm):
    t, k = x.shape
    n = w.shape[1]
    row = lambda i: (i, 0)
    fixed = lambda i: (0, 0)
    return pl.pallas_call(
        _mla_in_kernel,
        grid=(t // bm,),
        in_specs=[pl.BlockSpec((bm, k), row),
                  pl.BlockSpec((k, n), fixed),
                  pl.BlockSpec((1, MLA_Q_RANK), fixed),
                  pl.BlockSpec((1, MLA_KV_RANK), fixed),
                  pl.BlockSpec((1, bm, LANES), lambda i: (1, i, 0)),
                  pl.BlockSpec((1, bm, LANES), lambda i: (1, i, 0))],
        out_specs=[pl.BlockSpec((bm, MLA_Q_RANK), row),
                   pl.BlockSpec((bm, MLA_KV_RANK), row),
                   pl.BlockSpec((bm, LANES), row)],
        out_shape=[jax.ShapeDtypeStruct((t, MLA_Q_RANK), BF16),
                   jax.ShapeDtypeStruct((t, MLA_KV_RANK), BF16),
                   jax.ShapeDtypeStruct((t, LANES), BF16)],
        compiler_params=_params(("parallel",)),
        name="mla_in",
    )(x, w, gq, gkv, cos_tab, sin_tab)


def _mm_res_ln_kernel(a_ref, w_ref, res_ref, g_ref, b_ref, of_ref, *maybe_ob_ref):
    acc = jnp.dot(a_ref[...], w_ref[...], preferred_element_type=F32)
    y = DN_ALPHA * res_ref[...] + acc
    mu = jnp.mean(y, axis=-1, keepdims=True)
    d = y - mu
    var = jnp.mean(d * d, axis=-1, keepdims=True)
    out = d * lax.rsqrt(var + LN_EPS) * g_ref[...] + b_ref[...]
    of_ref[...] = out
    for ob_ref in maybe_ob_ref:
        ob_ref[...] = out.astype(ob_ref.dtype)


def _mm_res_ln(a, w, res, g, b, *, bm, want_bf16, name):
    t, kdim = a.shape
    n = w.shape[1]
    row = lambda i: (i, 0)
    fixed = lambda i: (0, 0)
    out_specs = [pl.BlockSpec((bm, n), row)]
    out_shape = [jax.ShapeDtypeStruct((t, n), F32)]
    if want_bf16:
        out_specs.append(pl.BlockSpec((bm, n), row))
        out_shape.append(jax.ShapeDtypeStruct((t, n), BF16))
    outs = pl.pallas_call(
        _mm_res_ln_kernel,
        grid=(t // bm,),
        in_specs=[pl.BlockSpec((bm, kdim), row),
                  pl.BlockSpec((kdim, n), fixed, pipeline_mode=pl.Buffered(1)),
                  pl.BlockSpec((bm, n), row),
                  pl.BlockSpec((1, n), fixed),
                  pl.BlockSpec((1, n), fixed)],
        out_specs=out_specs,
        out_shape=out_shape,
        compiler_params=_params(("arbitrary",)),
        name=name,
    )(a, w, res, g, b)
    return outs if want_bf16 else (outs[0], None)


def _ffn_up_kernel(x_ref, wg_ref, wu_ref, o_ref, wgb_ref, wub_ref):
    @pl.when(pl.program_id(1) == 0)
    def _():
        wgb_ref[...] = wg_ref[...].astype(BF16)
        wub_ref[...] = wu_ref[...].astype(BF16)

    x = x_ref[...]
    gate = jnp.dot(x, wgb_ref[...], preferred_element_type=F32)
    up = jnp.dot(x, wub_ref[...], preferred_element_type=F32)
    o_ref[...] = (gate * jax.nn.sigmoid(gate) * up).astype(o_ref.dtype)


def _ffn_up(x, wg, wu, layer, *, bm, bn):
    t, k = x.shape
    n = wg.shape[2]
    return pl.pallas_call(
        _ffn_up_kernel,
        grid=(n // bn, t // bm),
        in_specs=[pl.BlockSpec((bm, k), lambda j, i: (i, 0)),
                  pl.BlockSpec((None, k, bn), lambda j, i: (layer, 0, j)),
                  pl.BlockSpec((None, k, bn), lambda j, i: (layer, 0, j))],
        out_specs=pl.BlockSpec((bm, bn), lambda j, i: (i, j)),
        out_shape=jax.ShapeDtypeStruct((t, n), BF16),
        scratch_shapes=[pltpu.VMEM((k, bn), BF16), pltpu.VMEM((k, bn), BF16)],
        compiler_params=_params(("arbitrary", "arbitrary")),
        name="ffn_up",
    )(x, wg, wu)


def _sb_kernel(q_ref, k_ref, v_ref, u_ref, o_ref, acc_ref, c_ref, za_ref, zb_ref, *, tq):
    tk = tq // 2
    i = pl.program_id(2)
    q = q_ref[...]
    u = u_ref[...]
    acc_ref[...] = jnp.zeros_like(acc_ref)
    c_ref[...] = jnp.zeros_like(c_ref)

    def logits(t):
        start = pl.multiple_of(t * tk, tk)
        return lax.dot_general(q, k_ref[pl.ds(start, tk), :], _NT, preferred_element_type=F32)

    def step(z, t, mask):
        start = pl.multiple_of(t * tk, tk)
        log_beta = jnp.minimum(z, 0.0) - jnp.log(1.0 + jnp.exp(-jnp.abs(z)))
        log_rest = log_beta - z
        if mask is not None:
            log_rest = jnp.where(mask, log_rest, 0.0)
        hi = log_rest.astype(BF16)
        lo = (log_rest - hi.astype(F32)).astype(BF16)
        later = jnp.dot(hi, u, preferred_element_type=F32) + jnp.dot(lo, u, preferred_element_type=F32)
        carry = c_ref[...]
        ws = []
        for c in range(tk // LANES):
            sl = slice(c * LANES, (c + 1) * LANES)
            w = jnp.exp(log_beta[:, sl] + later[:, sl] + carry)
            if mask is not None:
                w = jnp.where(mask[:, sl], w, 0.0)
            ws.append(w.astype(BF16))
        acc_ref[...] += jnp.dot(jnp.concatenate(ws, axis=-1), v_ref[pl.ds(start, tk), :],
                                preferred_element_type=F32)
        carry = carry + jnp.sum(log_rest, axis=-1, keepdims=True)
        c_ref[...] = carry
        return carry

    row = lax.broadcasted_iota(jnp.int32, (tq, tk), 0)
    col = lax.broadcasted_iota(jnp.int32, (tq, tk), 1)
    t = 2 * i + 1
    za_ref[...] = logits(t)
    zb_ref[...] = logits(t - 1)
    step(za_ref[...], t, col + tk < row)
    za_ref[...] = logits(jnp.maximum(t - 2, 0))
    carry = step(zb_ref[...], t - 1, col < row)

    def alive(state):
        p, top = state
        return (p <= i) & (top > SB_ZERO_LOG)

    def body(state):
        p, _ = state
        t = 2 * (i - p) + 1
        zb_ref[...] = logits(t - 1)
        step(za_ref[...], t, None)
        za_ref[...] = logits(jnp.maximum(t - 2, 0))
        carry = step(zb_ref[...], t - 1, None)
        return p + 1, jnp.max(carry)

    lax.while_loop(alive, body, (jnp.int32(1), jnp.max(carry)))
    o_ref[...] = acc_ref[...].astype(o_ref.dtype)


def _sb_attention(h, u, *, batch, seq, tq):
    nq = seq // tq
    nh = SB_HEADS
    kern = functools.partial(_sb_kernel, tq=tq)
    kcol = SB_WIDTH // HEAD_DIM
    return pl.pallas_call(
        kern,
        grid=(batch, nh, nq),
        in_specs=[pl.BlockSpec((tq, HEAD_DIM), lambda b, hh, i: (b * nq + i, hh)),
                  pl.BlockSpec((seq, HEAD_DIM), lambda b, hh, i: (b, kcol + hh)),
                  pl.BlockSpec((seq, HEAD_DIM), lambda b, hh, i: (b, 2 * kcol + hh)),
                  pl.BlockSpec((tq // 2, tq // 2), lambda b, hh, i: (0, 0))],
        out_specs=pl.BlockSpec((tq, HEAD_DIM), lambda b, hh, i: (b * nq + i, hh)),
        out_shape=jax.ShapeDtypeStruct((batch * seq, SB_WIDTH), BF16),
        scratch_shapes=[pltpu.VMEM((tq, HEAD_DIM), F32),
                        pltpu.VMEM((tq, LANES), F32),
                        pltpu.VMEM((tq, tq // 2), F32),
                        pltpu.VMEM((tq, tq // 2), F32)],
        compiler_params=_params(("parallel", "parallel", "arbitrary")),
        name="sb_attention",
    )(h, h, h, u)


def _softmax_step(s, vs, m_ref, l_ref, acc_ref):
    m_old = m_ref[...]
    m_new = jnp.maximum(m_old, jnp.max(s, axis=-1, keepdims=True))
    alpha = jnp.exp2(m_old - m_new)
    ps = [jnp.exp2(s[:, c * LANES:(c + 1) * LANES] - m_new) for c in range(s.shape[1] // LANES)]
    l_ref[...] = alpha * l_ref[...] + functools.reduce(lambda a, b: a + b, ps)
    pv = jnp.dot(jnp.concatenate(ps, axis=-1).astype(BF16), vs, preferred_element_type=F32)
    for c in range(pv.shape[1] // LANES):
        sl = slice(c * LANES, (c + 1) * LANES)
        acc_ref[:, sl] = alpha * acc_ref[:, sl] + pv[:, sl]
    m_ref[...] = m_new


def _causal_pair(tq, tk):
    row = lax.broadcasted_iota(jnp.int32, (tq, tk), 0)
    col = lax.broadcasted_iota(jnp.int32, (tq, tk), 1)
    return col <= row, col + tk <= row


def _diff_kernel(q_ref, k_ref, v_ref, lq1_ref, lk1_ref, lq2_ref, lk2_ref, g_ref, o_ref,
                 acc_ref, m_ref, l_ref, sa_ref, sb_ref, *, tq, lambda_init):
    tk = tq // 2
    i = pl.program_id(2)
    q = q_ref[...]
    m_ref[...] = jnp.full_like(m_ref, NEG_BIG)
    l_ref[...] = jnp.zeros_like(l_ref)
    acc_ref[...] = jnp.zeros_like(acc_ref)

    def scores(t, s_ref):
        start = pl.multiple_of(t * tk, tk)
        ks = k_ref[pl.ds(start, tk), :]
        for mp in range(2):
            sl = slice(mp * HEAD_DIM, (mp + 1) * HEAD_DIM)
            s_ref[mp] = lax.dot_general(q[:, sl], ks[:, sl], _NT, preferred_element_type=F32)

    def step(s_ref, t, mask):
        start = pl.multiple_of(t * tk, tk)
        vs = v_ref[pl.ds(start, tk), :]
        for mp in range(2):
            s = s_ref[mp]
            if mask is not None:
                s = jnp.where(mask, s, NEG_BIG)
            _softmax_step(s, vs, m_ref.at[mp], l_ref.at[mp], acc_ref.at[mp])

    scores(0, sa_ref)

    def body(p, c):
        t = 2 * p
        scores(t + 1, sb_ref)
        step(sa_ref, t, None)
        scores(t + 2, sa_ref)
        step(sb_ref, t + 1, None)
        return c

    lax.fori_loop(0, i, body, 0)
    t = 2 * i
    scores(t + 1, sb_ref)
    mask_a, mask_b = _causal_pair(tq, tk)
    step(sa_ref, t, mask_a)
    step(sb_ref, t + 1, mask_b)

    lam = (jnp.exp(jnp.sum(lq1_ref[...] * lk1_ref[...], axis=-1, keepdims=True))
           - jnp.exp(jnp.sum(lq2_ref[...] * lk2_ref[...], axis=-1, keepdims=True))
           + lambda_init)
    l0 = jnp.sum(l_ref[0], axis=-1, keepdims=True)
    l1 = jnp.sum(l_ref[1], axis=-1, keepdims=True)
    o = acc_ref[0] / l0 - lam * (acc_ref[1] / l1)
    o = _rms(o, g_ref[...], LN_EPS) * (1.0 - lambda_init)
    o_ref[...] = o.astype(o_ref.dtype)


def _diff_attention(h, lq1, lk1, lq2, lk2, g, *, batch, seq, tq, lambda_init):
    nq = seq // tq
    kern = functools.partial(_diff_kernel, tq=tq, lambda_init=lambda_init)
    qcol = 3 * SB_WIDTH // DIFF_V_DIM
    kcol = qcol + DIFF_QK_WIDTH // DIFF_V_DIM
    vcol = kcol + DIFF_QK_WIDTH // DIFF_V_DIM
    vec = pl.BlockSpec((1, HEAD_DIM), lambda b, hh, i: (0, 0))
    return pl.pallas_call(
        kern,
        grid=(batch, DIFF_HEADS, nq),
        in_specs=[pl.BlockSpec((tq, DIFF_V_DIM), lambda b, hh, i: (b * nq + i, qcol + hh)),
                  pl.BlockSpec((seq, DIFF_V_DIM), lambda b, hh, i: (b, kcol + hh)),
                  pl.BlockSpec((seq, DIFF_V_DIM), lambda b, hh, i: (b, vcol + hh)),
                  vec, vec, vec, vec,
                  pl.BlockSpec((1, DIFF_V_DIM), lambda b, hh, i: (0, 0))],
        out_specs=pl.BlockSpec((tq, DIFF_V_DIM), lambda b, hh, i: (b * nq + i, hh)),
        out_shape=jax.ShapeDtypeStruct((batch * seq, DIFF_V_WIDTH), BF16),
        scratch_shapes=[pltpu.VMEM((2, tq, DIFF_V_DIM), F32),
                        pltpu.VMEM((2, tq, LANES), F32),
                        pltpu.VMEM((2, tq, LANES), F32),
                        pltpu.VMEM((2, tq, tq // 2), F32),
                        pltpu.VMEM((2, tq, tq // 2), F32)],
        compiler_params=_params(("parallel", "parallel", "arbitrary")),
        name="diff_attention",
    )(h, h, h, lq1, lk1, lq2, lk2, g)


def _mla_kernel(q_ref, kv_ref, kpe_ref, o_ref, acc_ref, m_ref, l_ref, sa_ref, sb_ref, *, tq, nh):
    tk = tq // 2
    wh = 2 * LANES
    i = pl.program_id(2)
    m_ref[...] = jnp.full_like(m_ref, NEG_BIG)
    l_ref[...] = jnp.zeros_like(l_ref)
    acc_ref[...] = jnp.zeros_like(acc_ref)

    def scores(t, s_ref):
        start = pl.multiple_of(t * tk, tk)
        kpe = kpe_ref[pl.ds(start, tk), :]
        for h in range(nh):
            ks = jnp.concatenate([kv_ref[pl.ds(start, tk), h * wh:h * wh + LANES], kpe], axis=-1)
            s_ref[h] = lax.dot_general(q_ref[:, h * wh:(h + 1) * wh], ks, _NT,
                                       preferred_element_type=F32)

    def step(s_ref, t, mask):
        start = pl.multiple_of(t * tk, tk)
        for h in range(nh):
            s = s_ref[h]
            if mask is not None:
                s = jnp.where(mask, s, NEG_BIG)
            _softmax_step(s, kv_ref[pl.ds(start, tk), h * wh + LANES:(h + 1) * wh],
                          m_ref.at[h], l_ref.at[h], acc_ref.at[h])

    scores(0, sa_ref)

    def body(p, c):
        t = 2 * p
        scores(t + 1, sb_ref)
        step(sa_ref, t, None)
        scores(t + 2, sa_ref)
        step(sb_ref, t + 1, None)
        return c

    lax.fori_loop(0, i, body, 0)
    t = 2 * i
    scores(t + 1, sb_ref)
    mask_a, mask_b = _causal_pair(tq, tk)
    step(sa_ref, t, mask_a)
    step(sb_ref, t + 1, mask_b)
    for h in range(nh):
        l = jnp.sum(l_ref[h], axis=-1, keepdims=True)
        o_ref[:, h * MLA_V_DIM:(h + 1) * MLA_V_DIM] = (acc_ref[h] / l).astype(o_ref.dtype)


def _mla_attention(q, kv, kpe, *, batch, seq, tq, nh):
    nq = seq // tq
    tk = tq // 2
    kern = functools.partial(_mla_kernel, tq=tq, nh=nh)
    return pl.pallas_call(
        kern,
        grid=(batch, MLA_HEADS // nh, nq),
        in_specs=[pl.BlockSpec((tq, nh * 2 * LANES), lambda b, hp, i: (b * nq + i, hp)),
                  pl.BlockSpec((seq, nh * 2 * LANES), lambda b, hp, i: (b, hp)),
                  pl.BlockSpec((seq, LANES), lambda b, hp, i: (b, 0))],
        out_specs=pl.BlockSpec((tq, nh * MLA_V_DIM), lambda b, hp, i: (b * nq + i, hp)),
        out_shape=jax.ShapeDtypeStruct((batch * seq, MLA_HEADS * MLA_V_DIM), BF16),
        scratch_shapes=[pltpu.VMEM((nh, tq, MLA_V_DIM), F32),
                        pltpu.VMEM((nh, tq, LANES), F32),
                        pltpu.VMEM((nh, tq, LANES), F32),
                        pltpu.VMEM((nh, tq, tk), F32),
                        pltpu.VMEM((nh, tq, tk), F32)],
        compiler_params=_params(("parallel", "parallel", "arbitrary")),
        name="mla_attention",
    )(q, kv, kpe)


def _spread_rope_cols(w):
    half = MLA_ROPE_DIM // 2
    z = jnp.zeros(w.shape[:-1] + (LANES // 2 - half,), w.dtype)
    return jnp.concatenate([w[..., :half], z, w[..., half:], z], axis=-1)


def _rope_freqs():
    f_full = ROPE_THETA ** (-jnp.arange(0, HEAD_DIM, 2, dtype=F32) / HEAD_DIM)
    f_mla = ROPE_THETA ** (-jnp.arange(0, MLA_ROPE_DIM, 2, dtype=F32) / MLA_ROPE_DIM)
    row_full = jnp.concatenate([f_full, f_full])
    row_mla = _spread_rope_cols(jnp.concatenate([f_mla, f_mla]))
    sign = jnp.concatenate([-jnp.ones((1, LANES // 2), F32), jnp.ones((1, LANES // 2), F32)], axis=-1)
    return jnp.stack([row_full, row_mla]), sign


def kernel(x, positions, sb_diff_w_in, sb_diff_w_out, diff_lambda_q1, diff_lambda_k1,
           diff_lambda_q2, diff_lambda_k2, diff_subln_g, mla_w_in, mla_q_norm_g, mla_kv_norm_g,
           mla_w_q_up, mla_w_kv_up, mla_w_out, ffn_w_gate, ffn_w_up, ffn_w_down, ln_g, ln_b):
    batch, seq, d = x.shape
    t = batch * seq
    tq = min(seq, 512)
    sub = tq // 2
    bm = min(t, 1024)
    bm_ln = min(t, 512)

    freqs, sign = _rope_freqs()
    cos_tab, sin_tab = _rope_tables(positions.reshape(t, 1), freqs, sign)
    u = (lax.broadcasted_iota(jnp.int32, (sub, sub), 0)
         > lax.broadcasted_iota(jnp.int32, (sub, sub), 1)).astype(BF16)

    xf = x.reshape(t, d)
    xb = xf.astype(BF16)

    def ffn(layer, xf, xb, want_bf16):
        h = _ffn_up(xb, ffn_w_gate, ffn_w_up, layer, bm=bm, bn=512)
        return _mm_res_ln(h, ffn_w_down[layer].astype(BF16), xf, ln_g[layer, 1][None], ln_b[layer, 1][None],
                          bm=256, want_bf16=want_bf16, name=f"ffn_down{layer}")

    h0 = _in_proj0(xb, sb_diff_w_in[0], cos_tab, sin_tab, bm=bm, bn=512)
    oa = _sb_attention(h0, u, batch=batch, seq=seq, tq=tq)
    lambda_init = 0.8 - 0.6 * math.exp(-0.3 * 0)
    od = _diff_attention(h0, diff_lambda_q1[0][None], diff_lambda_k1[0][None], diff_lambda_q2[0][None],
                         diff_lambda_k2[0][None], diff_subln_g[0][None],
                         batch=batch, seq=seq, tq=tq, lambda_init=lambda_init)
    mix = jnp.concatenate([oa, od], axis=-1)
    xf, xb = _mm_res_ln(mix, sb_diff_w_out[0].astype(BF16), xf, ln_g[0, 0][None], ln_b[0, 0][None],
                        bm=bm_ln, want_bf16=True, name="attn_out0")
    xf, xb = ffn(0, xf, xb, True)

    w_in1 = jnp.concatenate([mla_w_in[0][:, :MLA_Q_RANK + MLA_KV_RANK],
                             _spread_rope_cols(mla_w_in[0][:, MLA_Q_RANK + MLA_KV_RANK:])],
                            axis=-1).astype(BF16)
    wq = mla_w_q_up[0].reshape(MLA_Q_RANK, MLA_HEADS, MLA_QK_DIM)
    wq = jnp.concatenate([wq[..., :MLA_NOPE_DIM], _spread_rope_cols(wq[..., MLA_NOPE_DIM:])], axis=-1)
    wq = wq.reshape(MLA_Q_RANK, MLA_HEADS * 2 * LANES).astype(BF16)
    cq, ckv, kpe = _mla_in(xb, w_in1, mla_q_norm_g[0][None], mla_kv_norm_g[0][None], cos_tab, sin_tab,
                           bm=bm_ln)
    q1 = _mla_q_up(cq, wq, cos_tab, sin_tab, bm=bm, bn=1024)
    kv1 = _mm(ckv, mla_w_kv_up[0].astype(BF16), bm=bm, bn=1024, name="mla_kv_up")
    o1 = _mla_attention(q1, kv1, kpe, batch=batch, seq=seq, tq=tq, nh=2)
    xf, xb = _mm_res_ln(o1, mla_w_out[0].astype(BF16), xf, ln_g[1, 0][None], ln_b[1, 0][None],
                        bm=bm_ln, want_bf16=True, name="attn_out1")
    xf, _ = ffn(1, xf, xb, False)
    return xf.reshape(batch, seq, d)
```

```python
import functools
import math

import jax
import jax.numpy as jnp
from jax import lax
from jax.experimental import pallas as pl
from jax.experimental.pallas import tpu as pltpu

F32 = jnp.float32
BF16 = jnp.bfloat16

D_MODEL = 2048
DEPTH = 2
HEAD_DIM = 128
SB_HEADS = 8
DIFF_HEADS = 4
DIFF_V_DIM = 256
SB_WIDTH = SB_HEADS * HEAD_DIM
DIFF_QK_WIDTH = DIFF_HEADS * 2 * HEAD_DIM
DIFF_V_WIDTH = DIFF_HEADS * DIFF_V_DIM
EVEN_IN_WIDTH = 3 * SB_WIDTH + 2 * DIFF_QK_WIDTH + DIFF_V_WIDTH
MLA_HEADS = 16
MLA_Q_RANK = 512
MLA_KV_RANK = 512
MLA_NOPE_DIM = 128
MLA_ROPE_DIM = 64
MLA_V_DIM = 128
MLA_QK_DIM = MLA_NOPE_DIM + MLA_ROPE_DIM
FFN_DIM = 5632
ROPE_THETA = 10000.0
LN_EPS = 1e-5
RMS_EPS = 1e-6
DN_ALPHA = (2 * DEPTH) ** 0.25

LANES = 128
V7X_VMEM_LIMIT = 52 * 1024 * 1024
NEG_BIG = -1e30
LOG2E = math.log2(math.e)
SB_ZERO_LOG = -110.0

_NT = (((1,), (1,)), ((), ()))


def _params(sem):
    return pltpu.CompilerParams(dimension_semantics=sem, vmem_limit_bytes=V7X_VMEM_LIMIT)


def _rope(x, cos, sin_signed):
    return x * cos + pltpu.roll(x, 64, 1) * sin_signed


def _rope_tables_kernel(pos_ref, freq_ref, sign_ref, cos_ref, sin_ref):
    ang = pos_ref[...].astype(F32) * freq_ref[0]
    cos_ref[0] = jnp.cos(ang)
    sin_ref[0] = jnp.sin(ang) * sign_ref[...]


def _rope_tables(pos_col, freqs, sign):
    t = pos_col.shape[0]
    bt = min(t, 1024)
    nf = freqs.shape[0]
    return pl.pallas_call(
        _rope_tables_kernel,
        grid=(nf, t // bt),
        in_specs=[pl.BlockSpec((bt, 1), lambda f, i: (i, 0)),
                  pl.BlockSpec((1, 1, LANES), lambda f, i: (f, 0, 0)),
                  pl.BlockSpec((1, LANES), lambda f, i: (0, 0))],
        out_specs=[pl.BlockSpec((1, bt, LANES), lambda f, i: (f, i, 0)),
                   pl.BlockSpec((1, bt, LANES), lambda f, i: (f, i, 0))],
        out_shape=[jax.ShapeDtypeStruct((nf, t, LANES), F32)] * 2,
        compiler_params=_params(("arbitrary", "arbitrary")),
        name="rope_tables",
    )(pos_col, freqs.reshape(nf, 1, LANES), sign)


def _in_proj0_kernel(x_ref, w_ref, cos_ref, sin_ref, o_ref, wb_ref, *, bn, scale):
    @pl.when(pl.program_id(1) == 0)
    def _():
        wb_ref[...] = w_ref[...].astype(BF16)

    col = pl.program_id(0) * bn
    is_rope = (col >= 3 * SB_WIDTH) & (col < 3 * SB_WIDTH + 2 * DIFF_QK_WIDTH)
    is_sb_q = col < SB_WIDTH
    is_diff_q = (col >= 3 * SB_WIDTH) & (col < 3 * SB_WIDTH + DIFF_QK_WIDTH)
    sc = jnp.where(is_sb_q, scale, jnp.where(is_diff_q, scale * LOG2E, 1.0)).astype(F32)

    @pl.when(is_rope)
    def _():
        a = cos_ref[0] * sc
        b = sin_ref[0] * sc
        acc = jnp.dot(x_ref[...], wb_ref[...], preferred_element_type=F32)
        for g in range(bn // LANES):
            blk = acc[:, g * LANES:(g + 1) * LANES]
            o_ref[:, g * LANES:(g + 1) * LANES] = _rope(blk, a, b).astype(o_ref.dtype)

    @pl.when(jnp.logical_not(is_rope))
    def _():
        acc = jnp.dot(x_ref[...], wb_ref[...], preferred_element_type=F32)
        o_ref[...] = (acc * sc).astype(o_ref.dtype)


def _in_proj0(x, w, cos_tab, sin_tab, *, bm, bn):
    t, k = x.shape
    n = w.shape[1]
    kern = functools.partial(_in_proj0_kernel, bn=bn, scale=HEAD_DIM ** -0.5)
    return pl.pallas_call(
        kern,
        grid=(n // bn, t // bm),
        in_specs=[pl.BlockSpec((bm, k), lambda j, i: (i, 0)),
                  pl.BlockSpec((k, bn), lambda j, i: (0, j)),
                  pl.BlockSpec((1, bm, LANES), lambda j, i: (0, i, 0)),
                  pl.BlockSpec((1, bm, LANES), lambda j, i: (0, i, 0))],
        out_specs=pl.BlockSpec((bm, bn), lambda j, i: (i, j)),
        out_shape=jax.ShapeDtypeStruct((t, n), BF16),
        scratch_shapes=[pltpu.VMEM((k, bn), BF16)],
        compiler_params=_params(("arbitrary", "arbitrary")),
        name="in_proj0",
    )(x, w, cos_tab, sin_tab)


def _rms(x, g, eps):
    return x * lax.rsqrt(jnp.mean(x * x, axis=-1, keepdims=True) + eps) * g


def _mla_proj_kernel(x_ref, w_ref, gq_ref, gkv_ref, cos_ref, sin_ref, wq_ref, wkv_ref,
                     q_ref, kv_ref, kpe_ref, *, scale, chunk):
    acc = jnp.dot(x_ref[...], w_ref[...], preferred_element_type=F32)
    cq = _rms(acc[:, :MLA_Q_RANK], gq_ref[...], RMS_EPS).astype(BF16)
    ckv = _rms(acc[:, MLA_Q_RANK:MLA_Q_RANK + MLA_KV_RANK], gkv_ref[...], RMS_EPS).astype(BF16)
    cos = cos_ref[0]
    sin = sin_ref[0]
    kpe_ref[...] = _rope(acc[:, MLA_Q_RANK + MLA_KV_RANK:], cos, sin).astype(kpe_ref.dtype)
    cos_q = cos * scale
    sin_q = sin * scale
    for c in range(q_ref.shape[1] // chunk):
        qc = jnp.dot(cq, wq_ref[:, c * chunk:(c + 1) * chunk], preferred_element_type=F32)
        for g in range(chunk // LANES):
            blk = qc[:, g * LANES:(g + 1) * LANES]
            blk = _rope(blk, cos_q, sin_q) if g % 2 == 1 else blk * scale
            q_ref[:, c * chunk + g * LANES:c * chunk + (g + 1) * LANES] = blk.astype(q_ref.dtype)
    for c in range(kv_ref.shape[1] // chunk):
        sl = slice(c * chunk, (c + 1) * chunk)
        kv_ref[:, sl] = jnp.dot(ckv, wkv_ref[:, sl], preferred_element_type=F32).astype(kv_ref.dtype)


def _mla_proj(x, w, gq, gkv, cos_tab, sin_tab, wq, wkv, *, bm):
    t, k = x.shape
    n = w.shape[1]
    nq = wq.shape[1]
    nkv = wkv.shape[1]
    row = lambda i: (i, 0)
    fixed = lambda i: (0, 0)
    once = pl.Buffered(1)
    kern = functools.partial(_mla_proj_kernel, scale=MLA_QK_DIM ** -0.5 * LOG2E, chunk=1024)
    return pl.pallas_call(
        kern,
        grid=(t // bm,),
        in_specs=[pl.BlockSpec((bm, k), row),
                  pl.BlockSpec((k, n), fixed, pipeline_mode=once),
                  pl.BlockSpec((1, MLA_Q_RANK), fixed),
                  pl.BlockSpec((1, MLA_KV_RANK), fixed),
                  pl.BlockSpec((1, bm, LANES), lambda i: (1, i, 0)),
                  pl.BlockSpec((1, bm, LANES), lambda i: (1, i, 0)),
                  pl.BlockSpec((MLA_Q_RANK, nq), fixed, pipeline_mode=once),
                  pl.BlockSpec((MLA_KV_RANK, nkv), fixed, pipeline_mode=once)],
        out_specs=[pl.BlockSpec((bm, nq), row),
                   pl.BlockSpec((bm, nkv), row),
                   pl.BlockSpec((bm, LANES), row)],
        out_shape=[jax.ShapeDtypeStruct((t, nq), BF16),
                   jax.ShapeDtypeStruct((t, nkv), BF16),
                   jax.ShapeDtypeStruct((t, LANES), BF16)],
        compiler_params=_params(("arbitrary",)),
        name="mla_proj",
    )(x, w, gq, gkv, cos_tab, sin_tab, wq, wkv)


def _mm_res_ln_kernel(a_ref, w_ref, res_ref, g_ref, b_ref, of_ref, *maybe_ob_ref, row_parts):
    rows = a_ref.shape[0] // row_parts
    for r in range(row_parts):
        sl = slice(r * rows, (r + 1) * rows)
        acc = jnp.dot(a_ref[sl, :], w_ref[...], preferred_element_type=F32)
        y = DN_ALPHA * res_ref[sl, :] + acc
        mu = jnp.mean(y, axis=-1, keepdims=True)
        d = y - mu
        var = jnp.mean(d * d, axis=-1, keepdims=True)
        out = d * lax.rsqrt(var + LN_EPS) * g_ref[...] + b_ref[...]
        of_ref[sl, :] = out
        for ob_ref in maybe_ob_ref:
            ob_ref[sl, :] = out.astype(ob_ref.dtype)


def _mm_res_ln(a, w, res, g, b, *, bm, row_parts, want_bf16, name):
    t, kdim = a.shape
    n = w.shape[1]
    row = lambda i: (i, 0)
    fixed = lambda i: (0, 0)
    out_specs = [pl.BlockSpec((bm, n), row)]
    out_shape = [jax.ShapeDtypeStruct((t, n), F32)]
    if want_bf16:
        out_specs.append(pl.BlockSpec((bm, n), row))
        out_shape.append(jax.ShapeDtypeStruct((t, n), BF16))
    outs = pl.pallas_call(
        functools.partial(_mm_res_ln_kernel, row_parts=row_parts),
        grid=(t // bm,),
        in_specs=[pl.BlockSpec((bm, kdim), row),
                  pl.BlockSpec((kdim, n), fixed, pipeline_mode=pl.Buffered(1)),
                  pl.BlockSpec((bm, n), row),
                  pl.BlockSpec((1, n), fixed),
                  pl.BlockSpec((1, n), fixed)],
        out_specs=out_specs,
        out_shape=out_shape,
        compiler_params=_params(("arbitrary",)),
        name=name,
    )(a, w, res, g, b)
    return outs if want_bf16 else (outs[0], None)


def _ffn_up_kernel(x_ref, wg_ref, wu_ref, o_ref, wgb_ref, wub_ref):
    @pl.when(pl.program_id(1) == 0)
    def _():
        wgb_ref[...] = wg_ref[...].astype(BF16)
        wub_ref[...] = wu_ref[...].astype(BF16)

    x = x_ref[...]
    gate = jnp.dot(x, wgb_ref[...], preferred_element_type=F32)
    up = jnp.dot(x, wub_ref[...], preferred_element_type=F32)
    o_ref[...] = (gate * jax.nn.sigmoid(gate) * up).astype(o_ref.dtype)


def _ffn_up(x, wg, wu, layer, *, bm, bn):
    t, k = x.shape
    n = wg.shape[2]
    return pl.pallas_call(
        _ffn_up_kernel,
        grid=(n // bn, t // bm),
        in_specs=[pl.BlockSpec((bm, k), lambda j, i: (i, 0)),
                  pl.BlockSpec((None, k, bn), lambda j, i: (layer, 0, j)),
                  pl.BlockSpec((None, k, bn), lambda j, i: (layer, 0, j))],
        out_specs=pl.BlockSpec((bm, bn), lambda j, i: (i, j)),
        out_shape=jax.ShapeDtypeStruct((t, n), BF16),
        scratch_shapes=[pltpu.VMEM((k, bn), BF16), pltpu.VMEM((k, bn), BF16)],
        compiler_params=_params(("arbitrary", "arbitrary")),
        name="ffn_up",
    )(x, wg, wu)


def _sb_kernel(q_ref, k_ref, v_ref, u_ref, o_ref, acc_ref, c_ref, za_ref, zb_ref, *, tq):
    tk = tq // 2
    i = pl.program_id(2)
    q = q_ref[...]
    u = u_ref[...]
    acc_ref[...] = jnp.zeros_like(acc_ref)
    c_ref[...] = jnp.zeros_like(c_ref)

    def logits(t):
        start = pl.multiple_of(t * tk, tk)
        return lax.dot_general(q, k_ref[pl.ds(start, tk), :], _NT, preferred_element_type=F32)

    def step(z, t, mask):
        start = pl.multiple_of(t * tk, tk)
        log_beta = jnp.minimum(z, 0.0) - jnp.log(1.0 + jnp.exp(-jnp.abs(z)))
        log_rest = log_beta - z
        if mask is not None:
            log_rest = jnp.where(mask, log_rest, 0.0)
        hi = log_rest.astype(BF16)
        lo = (log_rest - hi.astype(F32)).astype(BF16)
        later = jnp.dot(hi, u, preferred_element_type=F32) + jnp.dot(lo, u, preferred_element_type=F32)
        carry = c_ref[...]
        ws = []
        for c in range(tk // LANES):
            sl = slice(c * LANES, (c + 1) * LANES)
            w = jnp.exp(log_beta[:, sl] + later[:, sl] + carry)
            if mask is not None:
                w = jnp.where(mask[:, sl], w, 0.0)
            ws.append(w.astype(BF16))
        acc_ref[...] += jnp.dot(jnp.concatenate(ws, axis=-1), v_ref[pl.ds(start, tk), :],
                                preferred_element_type=F32)
        carry = carry + jnp.sum(log_rest, axis=-1, keepdims=True)
        c_ref[...] = carry
        return carry

    row = lax.broadcasted_iota(jnp.int32, (tq, tk), 0)
    col = lax.broadcasted_iota(jnp.int32, (tq, tk), 1)
    t = 2 * i + 1
    za_ref[...] = logits(t)
    zb_ref[...] = logits(t - 1)
    step(za_ref[...], t, col + tk < row)
    za_ref[...] = logits(jnp.maximum(t - 2, 0))
    carry = step(zb_ref[...], t - 1, col < row)

    def alive(state):
        p, top = state
        return (p <= i) & (top > SB_ZERO_LOG)

    def body(state):
        p, _ = state
        t = 2 * (i - p) + 1
        zb_ref[...] = logits(t - 1)
        step(za_ref[...], t, None)
        za_ref[...] = logits(jnp.maximum(t - 2, 0))
        carry = step(zb_ref[...], t - 1, None)
        return p + 1, jnp.max(carry)

    lax.while_loop(alive, body, (jnp.int32(1), jnp.max(carry)))
    o_ref[...] = acc_ref[...].astype(o_ref.dtype)


def _sb_attention(h, u, *, batch, seq, tq):
    nq = seq // tq
    nh = SB_HEADS
    kern = functools.partial(_sb_kernel, tq=tq)
    kcol = SB_WIDTH // HEAD_DIM
    return pl.pallas_call(
        kern,
        grid=(batch, nh, nq),
        in_specs=[pl.BlockSpec((tq, HEAD_DIM), lambda b, hh, i: (b * nq + i, hh)),
                  pl.BlockSpec((seq, HEAD_DIM), lambda b, hh, i: (b, kcol + hh)),
                  pl.BlockSpec((seq, HEAD_DIM), lambda b, hh, i: (b, 2 * kcol + hh)),
                  pl.BlockSpec((tq // 2, tq // 2), lambda b, hh, i: (0, 0))],
        out_specs=pl.BlockSpec((tq, HEAD_DIM), lambda b, hh, i: (b * nq + i, hh)),
        out_shape=jax.ShapeDtypeStruct((batch * seq, SB_WIDTH), BF16),
        scratch_shapes=[pltpu.VMEM((tq, HEAD_DIM), F32),
                        pltpu.VMEM((tq, LANES), F32),
                        pltpu.VMEM((tq, tq // 2), F32),
                        pltpu.VMEM((tq, tq // 2), F32)],
        compiler_params=_params(("parallel", "parallel", "arbitrary")),
        name="sb_attention",
    )(h, h, h, u)


def _softmax_step(s, vs, m_ref, l_ref, acc_ref):
    m_old = m_ref[...]
    m_new = jnp.maximum(m_old, jnp.max(s, axis=-1, keepdims=True))
    alpha = jnp.exp2(m_old - m_new)
    ps = [jnp.exp2(s[:, c * LANES:(c + 1) * LANES] - m_new) for c in range(s.shape[1] // LANES)]
    l_ref[...] = alpha * l_ref[...] + functools.reduce(lambda a, b: a + b, ps)
    pv = jnp.dot(jnp.concatenate(ps, axis=-1).astype(BF16), vs, preferred_element_type=F32)
    for c in range(pv.shape[1] // LANES):
        sl = slice(c * LANES, (c + 1) * LANES)
        acc_ref[:, sl] = alpha * acc_ref[:, sl] + pv[:, sl]
    m_ref[...] = m_new


def _causal_pair(tq, tk):
    row = lax.broadcasted_iota(jnp.int32, (tq, tk), 0)
    col = lax.broadcasted_iota(jnp.int32, (tq, tk), 1)
    return col <= row, col + tk <= row


def _diff_kernel(q_ref, k_ref, v_ref, lq1_ref, lk1_ref, lq2_ref, lk2_ref, g_ref, o_ref,
                 acc_ref, m_ref, l_ref, sa_ref, sb_ref, *, tq, lambda_init):
    tk = tq // 2
    i = pl.program_id(2)
    q = q_ref[...]
    m_ref[...] = jnp.full_like(m_ref, NEG_BIG)
    l_ref[...] = jnp.zeros_like(l_ref)
    acc_ref[...] = jnp.zeros_like(acc_ref)

    def scores(t, s_ref):
        start = pl.multiple_of(t * tk, tk)
        ks = k_ref[pl.ds(start, tk), :]
        for mp in range(2):
            sl = slice(mp * HEAD_DIM, (mp + 1) * HEAD_DIM)
            s_ref[mp] = lax.dot_general(q[:, sl], ks[:, sl], _NT, preferred_element_type=F32)

    def step(s_ref, t, mask):
        start = pl.multiple_of(t * tk, tk)
        vs = v_ref[pl.ds(start, tk), :]
        for mp in range(2):
            s = s_ref[mp]
            if mask is not None:
                s = jnp.where(mask, s, NEG_BIG)
            _softmax_step(s, vs, m_ref.at[mp], l_ref.at[mp], acc_ref.at[mp])

    scores(0, sa_ref)

    def body(p, c):
        t = 2 * p
        scores(t + 1, sb_ref)
        step(sa_ref, t, None)
        scores(t + 2, sa_ref)
        step(sb_ref, t + 1, None)
        return c

    lax.fori_loop(0, i, body, 0)
    t = 2 * i
    scores(t + 1, sb_ref)
    mask_a, mask_b = _causal_pair(tq, tk)
    step(sa_ref, t, mask_a)
    step(sb_ref, t + 1, mask_b)

    lam = (jnp.exp(jnp.sum(lq1_ref[...] * lk1_ref[...], axis=-1, keepdims=True))
           - jnp.exp(jnp.sum(lq2_ref[...] * lk2_ref[...], axis=-1, keepdims=True))
           + lambda_init)
    l0 = jnp.sum(l_ref[0], axis=-1, keepdims=True)
    l1 = jnp.sum(l_ref[1], axis=-1, keepdims=True)
    o = acc_ref[0] / l0 - lam * (acc_ref[1] / l1)
    o = _rms(o, g_ref[...], LN_EPS) * (1.0 - lambda_init)
    o_ref[...] = o.astype(o_ref.dtype)


def _diff_attention(h, lq1, lk1, lq2, lk2, g, *, batch, seq, tq, lambda_init):
    nq = seq // tq
    kern = functools.partial(_diff_kernel, tq=tq, lambda_init=lambda_init)
    qcol = 3 * SB_WIDTH // DIFF_V_DIM
    kcol = qcol + DIFF_QK_WIDTH // DIFF_V_DIM
    vcol = kcol + DIFF_QK_WIDTH // DIFF_V_DIM
    vec = pl.BlockSpec((1, HEAD_DIM), lambda b, hh, i: (0, 0))
    return pl.pallas_call(
        kern,
        grid=(batch, DIFF_HEADS, nq),
        in_specs=[pl.BlockSpec((tq, DIFF_V_DIM), lambda b, hh, i: (b * nq + i, qcol + hh)),
                  pl.BlockSpec((seq, DIFF_V_DIM), lambda b, hh, i: (b, kcol + hh)),
                  pl.BlockSpec((seq, DIFF_V_DIM), lambda b, hh, i: (b, vcol + hh)),
                  vec, vec, vec, vec,
                  pl.BlockSpec((1, DIFF_V_DIM), lambda b, hh, i: (0, 0))],
        out_specs=pl.BlockSpec((tq, DIFF_V_DIM), lambda b, hh, i: (b * nq + i, hh)),
        out_shape=jax.ShapeDtypeStruct((batch * seq, DIFF_V_WIDTH), BF16),
        scratch_shapes=[pltpu.VMEM((2, tq, DIFF_V_DIM), F32),
                        pltpu.VMEM((2, tq, LANES), F32),
                        pltpu.VMEM((2, tq, LANES), F32),
                        pltpu.VMEM((2, tq, tq // 2), F32),
                        pltpu.VMEM((2, tq, tq // 2), F32)],
        compiler_params=_params(("parallel", "parallel", "arbitrary")),
        name="diff_attention",
    )(h, h, h, lq1, lk1, lq2, lk2, g)


def _mla_kernel(q_ref, kv_ref, kpe_ref, o_ref, acc_ref, m_ref, l_ref, sa_ref, sb_ref, *, tq, nh):
    tk = tq // 2
    wh = 2 * LANES
    i = pl.program_id(2)
    m_ref[...] = jnp.full_like(m_ref, NEG_BIG)
    l_ref[...] = jnp.zeros_like(l_ref)
    acc_ref[...] = jnp.zeros_like(acc_ref)

    def scores(t, s_ref):
        start = pl.multiple_of(t * tk, tk)
        kpe = kpe_ref[pl.ds(start, tk), :]
        for h in range(nh):
            ks = jnp.concatenate([kv_ref[pl.ds(start, tk), h * wh:h * wh + LANES], kpe], axis=-1)
            s_ref[h] = lax.dot_general(q_ref[:, h * wh:(h + 1) * wh], ks, _NT,
                                       preferred_element_type=F32)

    def step(s_ref, t, mask):
        start = pl.multiple_of(t * tk, tk)
        for h in range(nh):
            s = s_ref[h]
            if mask is not None:
                s = jnp.where(mask, s, NEG_BIG)
            _softmax_step(s, kv_ref[pl.ds(start, tk), h * wh + LANES:(h + 1) * wh],
                          m_ref.at[h], l_ref.at[h], acc_ref.at[h])

    scores(0, sa_ref)

    def body(p, c):
        t = 2 * p
        scores(t + 1, sb_ref)
        step(sa_ref, t, None)
        scores(t + 2, sa_ref)
        step(sb_ref, t + 1, None)
        return c

    lax.fori_loop(0, i, body, 0)
    t = 2 * i
    scores(t + 1, sb_ref)
    mask_a, mask_b = _causal_pair(tq, tk)
    step(sa_ref, t, mask_a)
    step(sb_ref, t + 1, mask_b)
    for h in range(nh):
        l = jnp.sum(l_ref[h], axis=-1, keepdims=True)
        o_ref[:, h * MLA_V_DIM:(h + 1) * MLA_V_DIM] = (acc_ref[h] / l).astype(o_ref.dtype)


def _mla_attention(q, kv, kpe, *, batch, seq, tq, nh):
    nq = seq // tq
    tk = tq // 2
    kern = functools.partial(_mla_kernel, tq=tq, nh=nh)
    return pl.pallas_call(
        kern,
        grid=(batch, MLA_HEADS // nh, nq),
        in_specs=[pl.BlockSpec((tq, nh * 2 * LANES), lambda b, hp, i: (b * nq + i, hp)),
                  pl.BlockSpec((seq, nh * 2 * LANES), lambda b, hp, i: (b, hp)),
                  pl.BlockSpec((seq, LANES), lambda b, hp, i: (b, 0))],
        out_specs=pl.BlockSpec((tq, nh * MLA_V_DIM), lambda b, hp, i: (b * nq + i, hp)),
        out_shape=jax.ShapeDtypeStruct((batch * seq, MLA_HEADS * MLA_V_DIM), BF16),
        scratch_shapes=[pltpu.VMEM((nh, tq, MLA_V_DIM), F32),
                        pltpu.VMEM((nh, tq, LANES), F32),
                        pltpu.VMEM((nh, tq, LANES), F32),
                        pltpu.VMEM((nh, tq, tk), F32),
                        pltpu.VMEM((nh, tq, tk), F32)],
        compiler_params=_params(("parallel", "parallel", "arbitrary")),
        name="mla_attention",
    )(q, kv, kpe)


def _spread_rope_cols(w):
    half = MLA_ROPE_DIM // 2
    z = jnp.zeros(w.shape[:-1] + (LANES // 2 - half,), w.dtype)
    return jnp.concatenate([w[..., :half], z, w[..., half:], z], axis=-1)


def _rope_freqs():
    f_full = ROPE_THETA ** (-jnp.arange(0, HEAD_DIM, 2, dtype=F32) / HEAD_DIM)
    f_mla = ROPE_THETA ** (-jnp.arange(0, MLA_ROPE_DIM, 2, dtype=F32) / MLA_ROPE_DIM)
    row_full = jnp.concatenate([f_full, f_full])
    row_mla = _spread_rope_cols(jnp.concatenate([f_mla, f_mla]))
    sign = jnp.concatenate([-jnp.ones((1, LANES // 2), F32), jnp.ones((1, LANES // 2), F32)], axis=-1)
    return jnp.stack([row_full, row_mla]), sign


def kernel(x, positions, sb_diff_w_in, sb_diff_w_out, diff_lambda_q1, diff_lambda_k1,
           diff_lambda_q2, diff_lambda_k2, diff_subln_g, mla_w_in, mla_q_norm_g, mla_kv_norm_g,
           mla_w_q_up, mla_w_kv_up, mla_w_out, ffn_w_gate, ffn_w_up, ffn_w_down, ln_g, ln_b):
    batch, seq, d = x.shape
    t = batch * seq
    tq = min(seq, 512)
    sub = tq // 2
    bm = min(t, 1024)
    bm_ln = min(t, 512)

    freqs, sign = _rope_freqs()
    cos_tab, sin_tab = _rope_tables(positions.reshape(t, 1), freqs, sign)
    u = (lax.broadcasted_iota(jnp.int32, (sub, sub), 0)
         > lax.broadcasted_iota(jnp.int32, (sub, sub), 1)).astype(BF16)

    xf = x.reshape(t, d)
    xb = xf.astype(BF16)

    def ffn(layer, xf, xb, want_bf16):
        h = _ffn_up(xb, ffn_w_gate, ffn_w_up, layer, bm=bm, bn=512)
        return _mm_res_ln(h, ffn_w_down[layer].astype(BF16), xf, ln_g[layer, 1][None], ln_b[layer, 1][None],
                          bm=256, row_parts=2, want_bf16=want_bf16, name=f"ffn_down{layer}")

    h0 = _in_proj0(xb, sb_diff_w_in[0], cos_tab, sin_tab, bm=bm, bn=512)
    oa = _sb_attention(h0, u, batch=batch, seq=seq, tq=tq)
    lambda_init = 0.8 - 0.6 * math.exp(-0.3 * 0)
    od = _diff_attention(h0, diff_lambda_q1[0][None], diff_lambda_k1[0][None], diff_lambda_q2[0][None],
                         diff_lambda_k2[0][None], diff_subln_g[0][None],
                         batch=batch, seq=seq, tq=tq, lambda_init=lambda_init)
    mix = jnp.concatenate([oa, od], axis=-1)
    xf, xb = _mm_res_ln(mix, sb_diff_w_out[0].astype(BF16), xf, ln_g[0, 0][None], ln_b[0, 0][None],
                        bm=bm_ln, row_parts=2, want_bf16=True, name="attn_out0")
    xf, xb = ffn(0, xf, xb, True)

    w_in1 = jnp.concatenate([mla_w_in[0][:, :MLA_Q_RANK + MLA_KV_RANK],
                             _spread_rope_cols(mla_w_in[0][:, MLA_Q_RANK + MLA_KV_RANK:])],
                            axis=-1).astype(BF16)
    wq = mla_w_q_up[0].reshape(MLA_Q_RANK, MLA_HEADS, MLA_QK_DIM)
    wq = jnp.concatenate([wq[..., :MLA_NOPE_DIM], _spread_rope_cols(wq[..., MLA_NOPE_DIM:])], axis=-1)
    wq = wq.reshape(MLA_Q_RANK, MLA_HEADS * 2 * LANES).astype(BF16)
    q1, kv1, kpe = _mla_proj(xb, w_in1, mla_q_norm_g[0][None], mla_kv_norm_g[0][None], cos_tab, sin_tab,
                             wq, mla_w_kv_up[0].astype(BF16), bm=bm_ln)
    o1 = _mla_attention(q1, kv1, kpe, batch=batch, seq=seq, tq=tq, nh=2)
    xf, xb = _mm_res_ln(o1, mla_w_out[0].astype(BF16), xf, ln_g[1, 0][None], ln_b[1, 0][None],
                        bm=bm_ln, row_parts=2, want_bf16=True, name="attn_out1")
    xf, _ = ffn(1, xf, xb, False)
    return xf.reshape(batch, seq, d)
```

```python
import functools
import math

import jax
import jax.numpy as jnp
from jax import lax
from jax.experimental import pallas as pl
from jax.experimental.pallas import tpu as pltpu

F32 = jnp.float32
BF16 = jnp.bfloat16

D_MODEL = 2048
DEPTH = 2
HEAD_DIM = 128
SB_HEADS = 8
DIFF_HEADS = 4
DIFF_V_DIM = 256
SB_WIDTH = SB_HEADS * HEAD_DIM
DIFF_QK_WIDTH = DIFF_HEADS * 2 * HEAD_DIM
DIFF_V_WIDTH = DIFF_HEADS * DIFF_V_DIM
EVEN_IN_WIDTH = 3 * SB_WIDTH + 2 * DIFF_QK_WIDTH + DIFF_V_WIDTH
MLA_HEADS = 16
MLA_Q_RANK = 512
MLA_KV_RANK = 512
MLA_NOPE_DIM = 128
MLA_ROPE_DIM = 64
MLA_V_DIM = 128
MLA_QK_DIM = MLA_NOPE_DIM + MLA_ROPE_DIM
FFN_DIM = 5632
ROPE_THETA = 10000.0
LN_EPS = 1e-5
RMS_EPS = 1e-6
DN_ALPHA = (2 * DEPTH) ** 0.25

LANES = 128
V7X_VMEM_LIMIT = 52 * 1024 * 1024
NEG_BIG = -1e30
LOG2E = math.log2(math.e)
SB_ZERO_LOG = -110.0

_NT = (((1,), (1,)), ((), ()))


def _params(sem):
    return pltpu.CompilerParams(dimension_semantics=sem, vmem_limit_bytes=V7X_VMEM_LIMIT)


def _rope(x, cos, sin_signed):
    return x * cos + pltpu.roll(x, 64, 1) * sin_signed


def _rope_tables_kernel(pos_ref, freq_ref, cos_ref, sin_ref):
    ang = pos_ref[...].astype(F32) * freq_ref[...]
    c = jnp.cos(ang)
    s = jnp.sin(ang)
    c_rot = pltpu.roll(c, 64, 1)
    s_rot = pltpu.roll(s, 64, 1)
    lane = lax.broadcasted_iota(jnp.int32, ang.shape, 1)
    low = lane < LANES // 2
    cos_ref[0] = jnp.where(low, c, c_rot)
    sin_ref[0] = jnp.where(low, -s, s_rot)
    half = MLA_ROPE_DIM // 2
    first = lane < half
    second = (lane >= LANES // 2) & (lane < LANES // 2 + half)
    cos_ref[1] = jnp.where(first, c_rot, jnp.where(second, c, 1.0))
    sin_ref[1] = jnp.where(first, -s_rot, jnp.where(second, s, 0.0))


def _rope_tables(pos_col, freq_row):
    t = pos_col.shape[0]
    bt = min(t, 1024)
    return pl.pallas_call(
        _rope_tables_kernel,
        grid=(t // bt,),
        in_specs=[pl.BlockSpec((bt, 1), lambda i: (i, 0)),
                  pl.BlockSpec((1, LANES), lambda i: (0, 0))],
        out_specs=[pl.BlockSpec((2, bt, LANES), lambda i: (0, i, 0)),
                   pl.BlockSpec((2, bt, LANES), lambda i: (0, i, 0))],
        out_shape=[jax.ShapeDtypeStruct((2, t, LANES), F32)] * 2,
        compiler_params=_params(("arbitrary",)),
        name="rope_tables",
    )(pos_col, freq_row)


def _in_proj0_kernel(x_ref, w_ref, cos_ref, sin_ref, o_ref, wb_ref, *, bn, scale):
    @pl.when(pl.program_id(1) == 0)
    def _():
        wb_ref[...] = w_ref[...].astype(BF16)

    col = pl.program_id(0) * bn
    is_rope = (col >= 3 * SB_WIDTH) & (col < 3 * SB_WIDTH + 2 * DIFF_QK_WIDTH)
    is_sb_q = col < SB_WIDTH
    is_diff_q = (col >= 3 * SB_WIDTH) & (col < 3 * SB_WIDTH + DIFF_QK_WIDTH)
    sc = jnp.where(is_sb_q, scale, jnp.where(is_diff_q, scale * LOG2E, 1.0)).astype(F32)

    @pl.when(is_rope)
    def _():
        a = cos_ref[0] * sc
        b = sin_ref[0] * sc
        acc = jnp.dot(x_ref[...], wb_ref[...], preferred_element_type=F32)
        for g in range(bn // LANES):
            blk = acc[:, g * LANES:(g + 1) * LANES]
            o_ref[:, g * LANES:(g + 1) * LANES] = _rope(blk, a, b).astype(o_ref.dtype)

    @pl.when(jnp.logical_not(is_rope))
    def _():
        acc = jnp.dot(x_ref[...], wb_ref[...], preferred_element_type=F32)
        o_ref[...] = (acc * sc).astype(o_ref.dtype)


def _in_proj0(x, w, cos_tab, sin_tab, *, bm, bn):
    t, k = x.shape
    n = w.shape[1]
    kern = functools.partial(_in_proj0_kernel, bn=bn, scale=HEAD_DIM ** -0.5)
    return pl.pallas_call(
        kern,
        grid=(n // bn, t // bm),
        in_specs=[pl.BlockSpec((bm, k), lambda j, i: (i, 0)),
                  pl.BlockSpec((k, bn), lambda j, i: (0, j)),
                  pl.BlockSpec((1, bm, LANES), lambda j, i: (0, i, 0)),
                  pl.BlockSpec((1, bm, LANES), lambda j, i: (0, i, 0))],
        out_specs=pl.BlockSpec((bm, bn), lambda j, i: (i, j)),
        out_shape=jax.ShapeDtypeStruct((t, n), BF16),
        scratch_shapes=[pltpu.VMEM((k, bn), BF16)],
        compiler_params=_params(("arbitrary", "arbitrary")),
        name="in_proj0",
    )(x, w, cos_tab, sin_tab)


def _rms(x, g, eps):
    return x * lax.rsqrt(jnp.mean(x * x, axis=-1, keepdims=True) + eps) * g


def _mla_proj_kernel(x_ref, w_ref, gq_ref, gkv_ref, cos_ref, sin_ref, wq_ref, wkv_ref,
                     q_ref, kv_ref, kpe_ref, *, scale, chunk):
    acc = jnp.dot(x_ref[...], w_ref[...], preferred_element_type=F32)
    cq = _rms(acc[:, :MLA_Q_RANK], gq_ref[...], RMS_EPS).astype(BF16)
    ckv = _rms(acc[:, MLA_Q_RANK:MLA_Q_RANK + MLA_KV_RANK], gkv_ref[...], RMS_EPS).astype(BF16)
    cos = cos_ref[0]
    sin = sin_ref[0]
    kpe_ref[...] = _rope(acc[:, MLA_Q_RANK + MLA_KV_RANK:], cos, sin).astype(kpe_ref.dtype)
    cos_q = cos * scale
    sin_q = sin * scale
    for c in range(q_ref.shape[1] // chunk):
        qc = jnp.dot(cq, wq_ref[:, c * chunk:(c + 1) * chunk], preferred_element_type=F32)
        for g in range(chunk // LANES):
            blk = qc[:, g * LANES:(g + 1) * LANES]
            blk = _rope(blk, cos_q, sin_q) if g % 2 == 1 else blk * scale
            q_ref[:, c * chunk + g * LANES:c * chunk + (g + 1) * LANES] = blk.astype(q_ref.dtype)
    for c in range(kv_ref.shape[1] // chunk):
        sl = slice(c * chunk, (c + 1) * chunk)
        kv_ref[:, sl] = jnp.dot(ckv, wkv_ref[:, sl], preferred_element_type=F32).astype(kv_ref.dtype)


def _mla_proj(x, w, gq, gkv, cos_tab, sin_tab, wq, wkv, *, bm):
    t, k = x.shape
    n = w.shape[1]
    nq = wq.shape[1]
    nkv = wkv.shape[1]
    row = lambda i: (i, 0)
    fixed = lambda i: (0, 0)
    once = pl.Buffered(1)
    kern = functools.partial(_mla_proj_kernel, scale=MLA_QK_DIM ** -0.5 * LOG2E, chunk=1024)
    return pl.pallas_call(
        kern,
        grid=(t // bm,),
        in_specs=[pl.BlockSpec((bm, k), row),
                  pl.BlockSpec((k, n), fixed, pipeline_mode=once),
                  pl.BlockSpec((1, MLA_Q_RANK), fixed),
                  pl.BlockSpec((1, MLA_KV_RANK), fixed),
                  pl.BlockSpec((1, bm, LANES), lambda i: (1, i, 0)),
                  pl.BlockSpec((1, bm, LANES), lambda i: (1, i, 0)),
                  pl.BlockSpec((MLA_Q_RANK, nq), fixed, pipeline_mode=once),
                  pl.BlockSpec((MLA_KV_RANK, nkv), fixed, pipeline_mode=once)],
        out_specs=[pl.BlockSpec((bm, nq), row),
                   pl.BlockSpec((bm, nkv), row),
                   pl.BlockSpec((bm, LANES), row)],
        out_shape=[jax.ShapeDtypeStruct((t, nq), BF16),
                   jax.ShapeDtypeStruct((t, nkv), BF16),
                   jax.ShapeDtypeStruct((t, LANES), BF16)],
        compiler_params=_params(("arbitrary",)),
        name="mla_proj",
    )(x, w, gq, gkv, cos_tab, sin_tab, wq, wkv)


def _mm_res_ln_kernel(a_ref, w_ref, res_ref, g_ref, b_ref, of_ref, *maybe_ob_ref, row_parts):
    rows = a_ref.shape[0] // row_parts
    for r in range(row_parts):
        sl = slice(r * rows, (r + 1) * rows)
        acc = jnp.dot(a_ref[sl, :], w_ref[...], preferred_element_type=F32)
        y = DN_ALPHA * res_ref[sl, :] + acc
        mu = jnp.mean(y, axis=-1, keepdims=True)
        d = y - mu
        var = jnp.mean(d * d, axis=-1, keepdims=True)
        out = d * lax.rsqrt(var + LN_EPS) * g_ref[...] + b_ref[...]
        of_ref[sl, :] = out
        for ob_ref in maybe_ob_ref:
            ob_ref[sl, :] = out.astype(ob_ref.dtype)


def _mm_res_ln(a, w, res, g, b, *, bm, row_parts, want_bf16, name):
    t, kdim = a.shape
    n = w.shape[1]
    row = lambda i: (i, 0)
    fixed = lambda i: (0, 0)
    out_specs = [pl.BlockSpec((bm, n), row)]
    out_shape = [jax.ShapeDtypeStruct((t, n), F32)]
    if want_bf16:
        out_specs.append(pl.BlockSpec((bm, n), row))
        out_shape.append(jax.ShapeDtypeStruct((t, n), BF16))
    outs = pl.pallas_call(
        functools.partial(_mm_res_ln_kernel, row_parts=row_parts),
        grid=(t // bm,),
        in_specs=[pl.BlockSpec((bm, kdim), row),
                  pl.BlockSpec((kdim, n), fixed, pipeline_mode=pl.Buffered(1)),
                  pl.BlockSpec((bm, n), row),
                  pl.BlockSpec((1, n), fixed),
                  pl.BlockSpec((1, n), fixed)],
        out_specs=out_specs,
        out_shape=out_shape,
        compiler_params=_params(("arbitrary",)),
        name=name,
    )(a, w, res, g, b)
    return outs if want_bf16 else (outs[0], None)


def _ffn_up_kernel(x_ref, wg_ref, wu_ref, o_ref, wgb_ref, wub_ref):
    @pl.when(pl.program_id(1) == 0)
    def _():
        wgb_ref[...] = wg_ref[...].astype(BF16)
        wub_ref[...] = wu_ref[...].astype(BF16)

    x = x_ref[...]
    gate = jnp.dot(x, wgb_ref[...], preferred_element_type=F32)
    up = jnp.dot(x, wub_ref[...], preferred_element_type=F32)
    o_ref[...] = (gate * jax.nn.sigmoid(gate) * up).astype(o_ref.dtype)


def _ffn_up(x, wg, wu, layer, *, bm, bn):
    t, k = x.shape
    n = wg.shape[2]
    return pl.pallas_call(
        _ffn_up_kernel,
        grid=(n // bn, t // bm),
        in_specs=[pl.BlockSpec((bm, k), lambda j, i: (i, 0)),
                  pl.BlockSpec((None, k, bn), lambda j, i: (layer, 0, j)),
                  pl.BlockSpec((None, k, bn), lambda j, i: (layer, 0, j))],
        out_specs=pl.BlockSpec((bm, bn), lambda j, i: (i, j)),
        out_shape=jax.ShapeDtypeStruct((t, n), BF16),
        scratch_shapes=[pltpu.VMEM((k, bn), BF16), pltpu.VMEM((k, bn), BF16)],
        compiler_params=_params(("arbitrary", "arbitrary")),
        name="ffn_up",
    )(x, wg, wu)


def _sb_kernel(q_ref, k_ref, v_ref, u_ref, o_ref, acc_ref, c_ref, za_ref, zb_ref, *, tq):
    tk = tq // 2
    i = pl.program_id(2)
    q = q_ref[...]
    u = u_ref[...]
    acc_ref[...] = jnp.zeros_like(acc_ref)
    c_ref[...] = jnp.zeros_like(c_ref)

    def logits(t, r0=0):
        start = pl.multiple_of(t * tk, tk)
        return lax.dot_general(q[r0:], k_ref[pl.ds(start, tk), :], _NT, preferred_element_type=F32)

    def step(z, t, mask, r0=0):
        rows = slice(r0, tq)
        start = pl.multiple_of(t * tk, tk)
        log_beta = jnp.minimum(z, 0.0) - jnp.log(1.0 + jnp.exp(-jnp.abs(z)))
        log_rest = log_beta - z
        if mask is not None:
            log_rest = jnp.where(mask, log_rest, 0.0)
        hi = log_rest.astype(BF16)
        lo = (log_rest - hi.astype(F32)).astype(BF16)
        later = jnp.dot(hi, u, preferred_element_type=F32) + jnp.dot(lo, u, preferred_element_type=F32)
        carry = c_ref[rows, :]
        ws = []
        for c in range(tk // LANES):
            sl = slice(c * LANES, (c + 1) * LANES)
            w = jnp.exp(log_beta[:, sl] + later[:, sl] + carry)
            if mask is not None:
                w = jnp.where(mask[:, sl], w, 0.0)
            ws.append(w.astype(BF16))
        acc_ref[rows, :] += jnp.dot(jnp.concatenate(ws, axis=-1), v_ref[pl.ds(start, tk), :],
                                    preferred_element_type=F32)
        carry = carry + jnp.sum(log_rest, axis=-1, keepdims=True)
        c_ref[rows, :] = carry
        return carry

    row = lax.broadcasted_iota(jnp.int32, (tq, tk), 0)
    col = lax.broadcasted_iota(jnp.int32, (tq, tk), 1)
    strict = col < row
    t = 2 * i + 1
    za_ref[tk:, :] = logits(t, tk)
    zb_ref[...] = logits(t - 1)
    step(za_ref[tk:, :], t, strict[:tq - tk], tk)
    za_ref[...] = logits(jnp.maximum(t - 2, 0))
    carry = step(zb_ref[...], t - 1, strict)

    def alive(state):
        p, top = state
        return (p <= i) & (top > SB_ZERO_LOG)

    def body(state):
        p, _ = state
        t = 2 * (i - p) + 1
        zb_ref[...] = logits(t - 1)
        top = jnp.max(step(za_ref[...], t, None))

        def second_tile():
            za_ref[...] = logits(jnp.maximum(t - 2, 0))
            return jnp.max(step(zb_ref[...], t - 1, None))

        return p + 1, lax.cond(top > SB_ZERO_LOG, second_tile, lambda: top)

    lax.while_loop(alive, body, (jnp.int32(1), jnp.max(carry)))
    o_ref[...] = acc_ref[...].astype(o_ref.dtype)


def _sb_attention(h, u, *, batch, seq, tq):
    nq = seq // tq
    nh = SB_HEADS
    kern = functools.partial(_sb_kernel, tq=tq)
    kcol = SB_WIDTH // HEAD_DIM
    return pl.pallas_call(
        kern,
        grid=(batch, nh, nq),
        in_specs=[pl.BlockSpec((tq, HEAD_DIM), lambda b, hh, i: (b * nq + i, hh)),
                  pl.BlockSpec((seq, HEAD_DIM), lambda b, hh, i: (b, kcol + hh)),
                  pl.BlockSpec((seq, HEAD_DIM), lambda b, hh, i: (b, 2 * kcol + hh)),
                  pl.BlockSpec((tq // 2, tq // 2), lambda b, hh, i: (0, 0))],
        out_specs=pl.BlockSpec((tq, HEAD_DIM), lambda b, hh, i: (b * nq + i, hh)),
        out_shape=jax.ShapeDtypeStruct((batch * seq, SB_WIDTH), BF16),
        scratch_shapes=[pltpu.VMEM((tq, HEAD_DIM), F32),
                        pltpu.VMEM((tq, LANES), F32),
                        pltpu.VMEM((tq, tq // 2), F32),
                        pltpu.VMEM((tq, tq // 2), F32)],
        compiler_params=_params(("parallel", "parallel", "arbitrary")),
        name="sb_attention",
    )(h, h, h, u)


def _softmax_step(s, vs, m_ref, l_ref, acc_ref):
    m_old = m_ref[...]
    m_new = jnp.maximum(m_old, jnp.max(s, axis=-1, keepdims=True))
    alpha = jnp.exp2(m_old - m_new)
    ps = [jnp.exp2(s[:, c * LANES:(c + 1) * LANES] - m_new) for c in range(s.shape[1] // LANES)]
    l_ref[...] = alpha * l_ref[...] + functools.reduce(lambda a, b: a + b, ps)
    pv = jnp.dot(jnp.concatenate(ps, axis=-1).astype(BF16), vs, preferred_element_type=F32)
    for c in range(pv.shape[1] // LANES):
        sl = slice(c * LANES, (c + 1) * LANES)
        acc_ref[:, sl] = alpha * acc_ref[:, sl] + pv[:, sl]
    m_ref[...] = m_new


def _causal_mask(tq, tk):
    row = lax.broadcasted_iota(jnp.int32, (tq, tk), 0)
    col = lax.broadcasted_iota(jnp.int32, (tq, tk), 1)
    return col <= row


def _diff_kernel(q_ref, k_ref, v_ref, lq1_ref, lk1_ref, lq2_ref, lk2_ref, g_ref, o_ref,
                 acc_ref, m_ref, l_ref, sa_ref, sb_ref, *, tq, lambda_init):
    tk = tq // 2
    i = pl.program_id(2)
    q = q_ref[...]
    m_ref[...] = jnp.full_like(m_ref, NEG_BIG)
    l_ref[...] = jnp.zeros_like(l_ref)
    acc_ref[...] = jnp.zeros_like(acc_ref)

    def scores(t, s_ref, r0=0):
        start = pl.multiple_of(t * tk, tk)
        ks = k_ref[pl.ds(start, tk), :]
        for mp in range(2):
            sl = slice(mp * HEAD_DIM, (mp + 1) * HEAD_DIM)
            s_ref[mp, r0:, :] = lax.dot_general(q[r0:, sl], ks[:, sl], _NT, preferred_element_type=F32)

    def step(s_ref, t, mask, r0=0):
        rows = pl.ds(r0, tq - r0)
        start = pl.multiple_of(t * tk, tk)
        vs = v_ref[pl.ds(start, tk), :]
        for mp in range(2):
            s = s_ref[mp, r0:, :]
            if mask is not None:
                s = jnp.where(mask, s, NEG_BIG)
            _softmax_step(s, vs, m_ref.at[mp, rows], l_ref.at[mp, rows], acc_ref.at[mp, rows])

    scores(0, sa_ref)

    def body(p, c):
        t = 2 * p
        scores(t + 1, sb_ref)
        step(sa_ref, t, None)
        scores(t + 2, sa_ref)
        step(sb_ref, t + 1, None)
        return c

    lax.fori_loop(0, i, body, 0)
    t = 2 * i
    scores(t + 1, sb_ref, tk)
    causal = _causal_mask(tq, tk)
    step(sa_ref, t, causal)
    step(sb_ref, t + 1, causal[:tq - tk], tk)

    lam = (jnp.exp(jnp.sum(lq1_ref[...] * lk1_ref[...], axis=-1, keepdims=True))
           - jnp.exp(jnp.sum(lq2_ref[...] * lk2_ref[...], axis=-1, keepdims=True))
           + lambda_init)
    l0 = jnp.sum(l_ref[0], axis=-1, keepdims=True)
    l1 = jnp.sum(l_ref[1], axis=-1, keepdims=True)
    o = acc_ref[0] / l0 - lam * (acc_ref[1] / l1)
    o = _rms(o, g_ref[...], LN_EPS) * (1.0 - lambda_init)
    o_ref[...] = o.astype(o_ref.dtype)


def _diff_attention(h, lq1, lk1, lq2, lk2, g, *, batch, seq, tq, lambda_init):
    nq = seq // tq
    kern = functools.partial(_diff_kernel, tq=tq, lambda_init=lambda_init)
    qcol = 3 * SB_WIDTH // DIFF_V_DIM
    kcol = qcol + DIFF_QK_WIDTH // DIFF_V_DIM
    vcol = kcol + DIFF_QK_WIDTH // DIFF_V_DIM
    vec = pl.BlockSpec((1, HEAD_DIM), lambda b, hh, i: (0, 0))
    return pl.pallas_call(
        kern,
        grid=(batch, DIFF_HEADS, nq),
        in_specs=[pl.BlockSpec((tq, DIFF_V_DIM), lambda b, hh, i: (b * nq + i, qcol + hh)),
                  pl.BlockSpec((seq, DIFF_V_DIM), lambda b, hh, i: (b, kcol + hh)),
                  pl.BlockSpec((seq, DIFF_V_DIM), lambda b, hh, i: (b, vcol + hh)),
                  vec, vec, vec, vec,
                  pl.BlockSpec((1, DIFF_V_DIM), lambda b, hh, i: (0, 0))],
        out_specs=pl.BlockSpec((tq, DIFF_V_DIM), lambda b, hh, i: (b * nq + i, hh)),
        out_shape=jax.ShapeDtypeStruct((batch * seq, DIFF_V_WIDTH), BF16),
        scratch_shapes=[pltpu.VMEM((2, tq, DIFF_V_DIM), F32),
                        pltpu.VMEM((2, tq, LANES), F32),
                        pltpu.VMEM((2, tq, LANES), F32),
                        pltpu.VMEM((2, tq, tq // 2), F32),
                        pltpu.VMEM((2, tq, tq // 2), F32)],
        compiler_params=_params(("parallel", "parallel", "arbitrary")),
        name="diff_attention",
    )(h, h, h, lq1, lk1, lq2, lk2, g)


def _mla_kernel(q_ref, kv_ref, kpe_ref, o_ref, acc_ref, m_ref, l_ref, sa_ref, sb_ref, *, tq, nh):
    tk = tq // 2
    wh = 2 * LANES
    i = pl.program_id(2)
    m_ref[...] = jnp.full_like(m_ref, NEG_BIG)
    l_ref[...] = jnp.zeros_like(l_ref)
    acc_ref[...] = jnp.zeros_like(acc_ref)

    def scores(t, s_ref, r0=0):
        start = pl.multiple_of(t * tk, tk)
        kpe = kpe_ref[pl.ds(start, tk), :]
        for h in range(nh):
            ks = jnp.concatenate([kv_ref[pl.ds(start, tk), h * wh:h * wh + LANES], kpe], axis=-1)
            s_ref[h, r0:, :] = lax.dot_general(q_ref[r0:, h * wh:(h + 1) * wh], ks, _NT,
                                               preferred_element_type=F32)

    def step(s_ref, t, mask, r0=0):
        rows = pl.ds(r0, tq - r0)
        start = pl.multiple_of(t * tk, tk)
        for h in range(nh):
            s = s_ref[h, r0:, :]
            if mask is not None:
                s = jnp.where(mask, s, NEG_BIG)
            _softmax_step(s, kv_ref[pl.ds(start, tk), h * wh + LANES:(h + 1) * wh],
                          m_ref.at[h, rows], l_ref.at[h, rows], acc_ref.at[h, rows])

    scores(0, sa_ref)

    def body(p, c):
        t = 2 * p
        scores(t + 1, sb_ref)
        step(sa_ref, t, None)
        scores(t + 2, sa_ref)
        step(sb_ref, t + 1, None)
        return c

    lax.fori_loop(0, i, body, 0)
    t = 2 * i
    scores(t + 1, sb_ref, tk)
    causal = _causal_mask(tq, tk)
    step(sa_ref, t, causal)
    step(sb_ref, t + 1, causal[:tq - tk], tk)
    for h in range(nh):
        l = jnp.sum(l_ref[h], axis=-1, keepdims=True)
        o_ref[:, h * MLA_V_DIM:(h + 1) * MLA_V_DIM] = (acc_ref[h] / l).astype(o_ref.dtype)


def _mla_attention(q, kv, kpe, *, batch, seq, tq, nh):
    nq = seq // tq
    tk = tq // 2
    kern = functools.partial(_mla_kernel, tq=tq, nh=nh)
    return pl.pallas_call(
        kern,
        grid=(batch, MLA_HEADS // nh, nq),
        in_specs=[pl.BlockSpec((tq, nh * 2 * LANES), lambda b, hp, i: (b * nq + i, hp)),
                  pl.BlockSpec((seq, nh * 2 * LANES), lambda b, hp, i: (b, hp)),
                  pl.BlockSpec((seq, LANES), lambda b, hp, i: (b, 0))],
        out_specs=pl.BlockSpec((tq, nh * MLA_V_DIM), lambda b, hp, i: (b * nq + i, hp)),
        out_shape=jax.ShapeDtypeStruct((batch * seq, MLA_HEADS * MLA_V_DIM), BF16),
        scratch_shapes=[pltpu.VMEM((nh, tq, MLA_V_DIM), F32),
                        pltpu.VMEM((nh, tq, LANES), F32),
                        pltpu.VMEM((nh, tq, LANES), F32),
                        pltpu.VMEM((nh, tq, tk), F32),
                        pltpu.VMEM((nh, tq, tk), F32)],
        compiler_params=_params(("parallel", "parallel", "arbitrary")),
        name="mla_attention",
    )(q, kv, kpe)


def _spread_rope_cols(w):
    half = MLA_ROPE_DIM // 2
    z = jnp.zeros(w.shape[:-1] + (LANES // 2 - half,), w.dtype)
    return jnp.concatenate([w[..., :half], z, w[..., half:], z], axis=-1)


def _rope_freq_row():
    f_full = ROPE_THETA ** (-jnp.arange(0, HEAD_DIM, 2, dtype=F32) / HEAD_DIM)
    f_mla = ROPE_THETA ** (-jnp.arange(0, MLA_ROPE_DIM, 2, dtype=F32) / MLA_ROPE_DIM)
    pad = jnp.zeros((LANES - f_full.shape[0] - f_mla.shape[0],), F32)
    return jnp.concatenate([f_full, f_mla, pad])[None]


def kernel(x, positions, sb_diff_w_in, sb_diff_w_out, diff_lambda_q1, diff_lambda_k1,
           diff_lambda_q2, diff_lambda_k2, diff_subln_g, mla_w_in, mla_q_norm_g, mla_kv_norm_g,
           mla_w_q_up, mla_w_kv_up, mla_w_out, ffn_w_gate, ffn_w_up, ffn_w_down, ln_g, ln_b):
    batch, seq, d = x.shape
    t = batch * seq
    tq = min(seq, 512)
    tq_soft = min(seq, 1024)
    sub = tq // 2
    bm = min(t, 1024)
    bm_ln = min(t, 512)

    cos_tab, sin_tab = _rope_tables(positions.reshape(t, 1), _rope_freq_row())
    u = (lax.broadcasted_iota(jnp.int32, (sub, sub), 0)
         > lax.broadcasted_iota(jnp.int32, (sub, sub), 1)).astype(BF16)

    xf = x.reshape(t, d)
    xb = xf.astype(BF16)

    def ffn(layer, xf, xb, want_bf16):
        h = _ffn_up(xb, ffn_w_gate, ffn_w_up, layer, bm=bm, bn=512)
        return _mm_res_ln(h, ffn_w_down[layer].astype(BF16), xf, ln_g[layer, 1][None], ln_b[layer, 1][None],
                          bm=256, row_parts=2, want_bf16=want_bf16, name=f"ffn_down{layer}")

    h0 = _in_proj0(xb, sb_diff_w_in[0], cos_tab, sin_tab, bm=bm, bn=512)
    oa = _sb_attention(h0, u, batch=batch, seq=seq, tq=tq)
    lambda_init = 0.8 - 0.6 * math.exp(-0.3 * 0)
    od = _diff_attention(h0, diff_lambda_q1[0][None], diff_lambda_k1[0][None], diff_lambda_q2[0][None],
                         diff_lambda_k2[0][None], diff_subln_g[0][None],
                         batch=batch, seq=seq, tq=tq_soft, lambda_init=lambda_init)
    mix = jnp.concatenate([oa, od], axis=-1)
    xf, xb = _mm_res_ln(mix, sb_diff_w_out[0].astype(BF16), xf, ln_g[0, 0][None], ln_b[0, 0][None],
                        bm=bm_ln, row_parts=2, want_bf16=True, name="attn_out0")
    xf, xb = ffn(0, xf, xb, True)

    w_in1 = jnp.concatenate([mla_w_in[0][:, :MLA_Q_RANK + MLA_KV_RANK],
                             _spread_rope_cols(mla_w_in[0][:, MLA_Q_RANK + MLA_KV_RANK:])],
                            axis=-1).astype(BF16)
    wq = mla_w_q_up[0].reshape(MLA_Q_RANK, MLA_HEADS, MLA_QK_DIM)
    wq = jnp.concatenate([wq[..., :MLA_NOPE_DIM], _spread_rope_cols(wq[..., MLA_NOPE_DIM:])], axis=-1)
    wq = wq.reshape(MLA_Q_RANK, MLA_HEADS * 2 * LANES).astype(BF16)
    q1, kv1, kpe = _mla_proj(xb, w_in1, mla_q_norm_g[0][None], mla_kv_norm_g[0][None], cos_tab, sin_tab,
                             wq, mla_w_kv_up[0].astype(BF16), bm=bm_ln)
    o1 = _mla_attention(q1, kv1, kpe, batch=batch, seq=seq, tq=tq_soft, nh=2)
    xf, xb = _mm_res_ln(o1, mla_w_out[0].astype(BF16), xf, ln_g[1, 0][None], ln_b[1, 0][None],
                        bm=bm_ln, row_parts=2, want_bf16=True, name="attn_out1")
    xf, _ = ffn(1, xf, xb, False)
    return xf.reshape(batch, seq, d)
```

```python
import functools
import math

import jax
import jax.numpy as jnp
from jax import lax
from jax.experimental import pallas as pl
from jax.experimental.pallas import tpu as pltpu

F32 = jnp.float32
BF16 = jnp.bfloat16

D_MODEL = 2048
DEPTH = 2
HEAD_DIM = 128
SB_HEADS = 8
DIFF_HEADS = 4
DIFF_V_DIM = 256
SB_WIDTH = SB_HEADS * HEAD_DIM
DIFF_QK_WIDTH = DIFF_HEADS * 2 * HEAD_DIM
DIFF_V_WIDTH = DIFF_HEADS * DIFF_V_DIM
EVEN_IN_WIDTH = 3 * SB_WIDTH + 2 * DIFF_QK_WIDTH + DIFF_V_WIDTH
MLA_HEADS = 16
MLA_Q_RANK = 512
MLA_KV_RANK = 512
MLA_NOPE_DIM = 128
MLA_ROPE_DIM = 64
MLA_V_DIM = 128
MLA_QK_DIM = MLA_NOPE_DIM + MLA_ROPE_DIM
FFN_DIM = 5632
ROPE_THETA = 10000.0
LN_EPS = 1e-5
RMS_EPS = 1e-6
DN_ALPHA = (2 * DEPTH) ** 0.25

LANES = 128
V7X_VMEM_LIMIT = 52 * 1024 * 1024
NEG_BIG = -1e30
LOG2E = math.log2(math.e)
SB_ZERO_LOG = -110.0

_NT = (((1,), (1,)), ((), ()))


def _params(sem):
    return pltpu.CompilerParams(dimension_semantics=sem, vmem_limit_bytes=V7X_VMEM_LIMIT)


def _rope(x, cos, sin_signed):
    return x * cos + pltpu.roll(x, 64, 1) * sin_signed


def _rope_tables_kernel(pos_ref, freq_ref, cos_ref, sin_ref):
    ang = pos_ref[...].astype(F32) * freq_ref[...]
    c = jnp.cos(ang)
    s = jnp.sin(ang)
    c_rot = pltpu.roll(c, 64, 1)
    s_rot = pltpu.roll(s, 64, 1)
    lane = lax.broadcasted_iota(jnp.int32, ang.shape, 1)
    low = lane < LANES // 2
    cos_ref[0] = jnp.where(low, c, c_rot)
    sin_ref[0] = jnp.where(low, -s, s_rot)
    half = MLA_ROPE_DIM // 2
    first = lane < half
    second = (lane >= LANES // 2) & (lane < LANES // 2 + half)
    cos_ref[1] = jnp.where(first, c_rot, jnp.where(second, c, 1.0))
    sin_ref[1] = jnp.where(first, -s_rot, jnp.where(second, s, 0.0))


def _rope_tables(pos_col, freq_row):
    t = pos_col.shape[0]
    bt = min(t, 1024)
    return pl.pallas_call(
        _rope_tables_kernel,
        grid=(t // bt,),
        in_specs=[pl.BlockSpec((bt, 1), lambda i: (i, 0)),
                  pl.BlockSpec((1, LANES), lambda i: (0, 0))],
        out_specs=[pl.BlockSpec((2, bt, LANES), lambda i: (0, i, 0)),
                   pl.BlockSpec((2, bt, LANES), lambda i: (0, i, 0))],
        out_shape=[jax.ShapeDtypeStruct((2, t, LANES), F32)] * 2,
        compiler_params=_params(("arbitrary",)),
        name="rope_tables",
    )(pos_col, freq_row)


def _in_proj0_kernel(x_ref, w_ref, cos_ref, sin_ref, o_ref, wb_ref, *, bn, scale):
    @pl.when(pl.program_id(1) == 0)
    def _():
        wb_ref[...] = w_ref[...].astype(BF16)

    col = pl.program_id(0) * bn
    is_rope = (col >= 3 * SB_WIDTH) & (col < 3 * SB_WIDTH + 2 * DIFF_QK_WIDTH)
    is_sb_q = col < SB_WIDTH
    is_diff_q = (col >= 3 * SB_WIDTH) & (col < 3 * SB_WIDTH + DIFF_QK_WIDTH)
    sc = jnp.where(is_sb_q, scale, jnp.where(is_diff_q, scale * LOG2E, 1.0)).astype(F32)

    @pl.when(is_rope)
    def _():
        a = cos_ref[0] * sc
        b = sin_ref[0] * sc
        acc = jnp.dot(x_ref[...], wb_ref[...], preferred_element_type=F32)
        for g in range(bn // LANES):
            blk = acc[:, g * LANES:(g + 1) * LANES]
            o_ref[:, g * LANES:(g + 1) * LANES] = _rope(blk, a, b).astype(o_ref.dtype)

    @pl.when(jnp.logical_not(is_rope))
    def _():
        acc = jnp.dot(x_ref[...], wb_ref[...], preferred_element_type=F32)
        o_ref[...] = (acc * sc).astype(o_ref.dtype)


def _in_proj0(x, w, cos_tab, sin_tab, *, bm, bn):
    t, k = x.shape
    n = w.shape[1]
    kern = functools.partial(_in_proj0_kernel, bn=bn, scale=HEAD_DIM ** -0.5)
    return pl.pallas_call(
        kern,
        grid=(n // bn, t // bm),
        in_specs=[pl.BlockSpec((bm, k), lambda j, i: (i, 0)),
                  pl.BlockSpec((k, bn), lambda j, i: (0, j)),
                  pl.BlockSpec((1, bm, LANES), lambda j, i: (0, i, 0)),
                  pl.BlockSpec((1, bm, LANES), lambda j, i: (0, i, 0))],
        out_specs=pl.BlockSpec((bm, bn), lambda j, i: (i, j)),
        out_shape=jax.ShapeDtypeStruct((t, n), BF16),
        scratch_shapes=[pltpu.VMEM((k, bn), BF16)],
        compiler_params=_params(("arbitrary", "arbitrary")),
        name="in_proj0",
    )(x, w, cos_tab, sin_tab)


def _rms(x, g, eps):
    return x * lax.rsqrt(jnp.mean(x * x, axis=-1, keepdims=True) + eps) * g


def _mla_proj_kernel(x_ref, w_ref, gq_ref, gkv_ref, cos_ref, sin_ref, wq_ref, wkv_ref,
                     q_ref, kv_ref, kpe_ref, *, scale, chunk):
    acc = jnp.dot(x_ref[...], w_ref[...], preferred_element_type=F32)
    cq = _rms(acc[:, :MLA_Q_RANK], gq_ref[...], RMS_EPS).astype(BF16)
    ckv = _rms(acc[:, MLA_Q_RANK:MLA_Q_RANK + MLA_KV_RANK], gkv_ref[...], RMS_EPS).astype(BF16)
    cos = cos_ref[0]
    sin = sin_ref[0]
    kpe_ref[...] = _rope(acc[:, MLA_Q_RANK + MLA_KV_RANK:], cos, sin).astype(kpe_ref.dtype)
    cos_q = cos * scale
    sin_q = sin * scale
    for c in range(q_ref.shape[1] // chunk):
        qc = jnp.dot(cq, wq_ref[:, c * chunk:(c + 1) * chunk], preferred_element_type=F32)
        for g in range(chunk // LANES):
            blk = qc[:, g * LANES:(g + 1) * LANES]
            blk = _rope(blk, cos_q, sin_q) if g % 2 == 1 else blk * scale
            q_ref[:, c * chunk + g * LANES:c * chunk + (g + 1) * LANES] = blk.astype(q_ref.dtype)
    for c in range(kv_ref.shape[1] // chunk):
        sl = slice(c * chunk, (c + 1) * chunk)
        kv_ref[:, sl] = jnp.dot(ckv, wkv_ref[:, sl], preferred_element_type=F32).astype(kv_ref.dtype)


def _mla_proj(x, w, gq, gkv, cos_tab, sin_tab, wq, wkv, *, bm):
    t, k = x.shape
    n = w.shape[1]
    nq = wq.shape[1]
    nkv = wkv.shape[1]
    row = lambda i: (i, 0)
    fixed = lambda i: (0, 0)
    once = pl.Buffered(1)
    kern = functools.partial(_mla_proj_kernel, scale=MLA_QK_DIM ** -0.5 * LOG2E, chunk=1024)
    return pl.pallas_call(
        kern,
        grid=(t // bm,),
        in_specs=[pl.BlockSpec((bm, k), row),
                  pl.BlockSpec((k, n), fixed, pipeline_mode=once),
                  pl.BlockSpec((1, MLA_Q_RANK), fixed),
                  pl.BlockSpec((1, MLA_KV_RANK), fixed),
                  pl.BlockSpec((1, bm, LANES), lambda i: (1, i, 0)),
                  pl.BlockSpec((1, bm, LANES), lambda i: (1, i, 0)),
                  pl.BlockSpec((MLA_Q_RANK, nq), fixed, pipeline_mode=once),
                  pl.BlockSpec((MLA_KV_RANK, nkv), fixed, pipeline_mode=once)],
        out_specs=[pl.BlockSpec((bm, nq), row),
                   pl.BlockSpec((bm, nkv), row),
                   pl.BlockSpec((bm, LANES), row)],
        out_shape=[jax.ShapeDtypeStruct((t, nq), BF16),
                   jax.ShapeDtypeStruct((t, nkv), BF16),
                   jax.ShapeDtypeStruct((t, LANES), BF16)],
        compiler_params=_params(("arbitrary",)),
        name="mla_proj",
    )(x, w, gq, gkv, cos_tab, sin_tab, wq, wkv)


def _mm_res_ln_kernel(a_ref, w_ref, res_ref, g_ref, b_ref, of_ref, *maybe_ob_ref, row_parts):
    rows = a_ref.shape[0] // row_parts
    for r in range(row_parts):
        sl = slice(r * rows, (r + 1) * rows)
        acc = jnp.dot(a_ref[sl, :], w_ref[...], preferred_element_type=F32)
        y = DN_ALPHA * res_ref[sl, :] + acc
        mu = jnp.mean(y, axis=-1, keepdims=True)
        d = y - mu
        var = jnp.mean(d * d, axis=-1, keepdims=True)
        out = d * lax.rsqrt(var + LN_EPS) * g_ref[...] + b_ref[...]
        of_ref[sl, :] = out
        for ob_ref in maybe_ob_ref:
            ob_ref[sl, :] = out.astype(ob_ref.dtype)


def _mm_res_ln(a, w, res, g, b, *, bm, row_parts, want_bf16, name):
    t, kdim = a.shape
    n = w.shape[1]
    row = lambda i: (i, 0)
    fixed = lambda i: (0, 0)
    out_specs = [pl.BlockSpec((bm, n), row)]
    out_shape = [jax.ShapeDtypeStruct((t, n), F32)]
    if want_bf16:
        out_specs.append(pl.BlockSpec((bm, n), row))
        out_shape.append(jax.ShapeDtypeStruct((t, n), BF16))
    outs = pl.pallas_call(
        functools.partial(_mm_res_ln_kernel, row_parts=row_parts),
        grid=(t // bm,),
        in_specs=[pl.BlockSpec((bm, kdim), row),
                  pl.BlockSpec((kdim, n), fixed, pipeline_mode=pl.Buffered(1)),
                  pl.BlockSpec((bm, n), row),
                  pl.BlockSpec((1, n), fixed),
                  pl.BlockSpec((1, n), fixed)],
        out_specs=out_specs,
        out_shape=out_shape,
        compiler_params=_params(("arbitrary",)),
        name=name,
    )(a, w, res, g, b)
    return outs if want_bf16 else (outs[0], None)


def _ffn_up_kernel(x_ref, wg_ref, wu_ref, o_ref, wgb_ref, wub_ref):
    @pl.when(pl.program_id(1) == 0)
    def _():
        wgb_ref[...] = wg_ref[...].astype(BF16)
        wub_ref[...] = wu_ref[...].astype(BF16)

    x = x_ref[...]
    gate = jnp.dot(x, wgb_ref[...], preferred_element_type=F32)
    up = jnp.dot(x, wub_ref[...], preferred_element_type=F32)
    o_ref[...] = (gate * jax.nn.sigmoid(gate) * up).astype(o_ref.dtype)


def _ffn_up(x, wg, wu, layer, *, bm, bn):
    t, k = x.shape
    n = wg.shape[2]
    return pl.pallas_call(
        _ffn_up_kernel,
        grid=(n // bn, t // bm),
        in_specs=[pl.BlockSpec((bm, k), lambda j, i: (i, 0)),
                  pl.BlockSpec((None, k, bn), lambda j, i: (layer, 0, j)),
                  pl.BlockSpec((None, k, bn), lambda j, i: (layer, 0, j))],
        out_specs=pl.BlockSpec((bm, bn), lambda j, i: (i, j)),
        out_shape=jax.ShapeDtypeStruct((t, n), BF16),
        scratch_shapes=[pltpu.VMEM((k, bn), BF16), pltpu.VMEM((k, bn), BF16)],
        compiler_params=_params(("arbitrary", "arbitrary")),
        name="ffn_up",
    )(x, wg, wu)


def _sb_kernel(q_ref, k_ref, v_ref, u_ref, o_ref, acc_ref, c_ref, za_ref, zb_ref, *, tq):
    tk = tq // 2
    i = pl.program_id(2)
    q = q_ref[...]
    u = u_ref[...]
    acc_ref[...] = jnp.zeros_like(acc_ref)
    c_ref[...] = jnp.zeros_like(c_ref)

    def logits(t, r0=0):
        start = pl.multiple_of(t * tk, tk)
        return lax.dot_general(q[r0:], k_ref[pl.ds(start, tk), :], _NT, preferred_element_type=F32)

    def step(z, t, mask, r0=0):
        rows = slice(r0, tq)
        start = pl.multiple_of(t * tk, tk)
        log_beta = jnp.minimum(z, 0.0) - jnp.log(1.0 + jnp.exp(-jnp.abs(z)))
        log_rest = log_beta - z
        if mask is not None:
            log_rest = jnp.where(mask, log_rest, 0.0)
        hi = log_rest.astype(BF16)
        lo = (log_rest - hi.astype(F32)).astype(BF16)
        later = jnp.dot(hi, u, preferred_element_type=F32) + jnp.dot(lo, u, preferred_element_type=F32)
        carry = c_ref[rows, :]
        ws = []
        for c in range(tk // LANES):
            sl = slice(c * LANES, (c + 1) * LANES)
            w = jnp.exp(log_beta[:, sl] + later[:, sl] + carry)
            if mask is not None:
                w = jnp.where(mask[:, sl], w, 0.0)
            ws.append(w.astype(BF16))
        acc_ref[rows, :] += jnp.dot(jnp.concatenate(ws, axis=-1), v_ref[pl.ds(start, tk), :],
                                    preferred_element_type=F32)
        carry = carry + jnp.sum(log_rest, axis=-1, keepdims=True)
        c_ref[rows, :] = carry
        return carry

    row = lax.broadcasted_iota(jnp.int32, (tq, tk), 0)
    col = lax.broadcasted_iota(jnp.int32, (tq, tk), 1)
    strict = col < row
    t = 2 * i + 1
    za_ref[tk:, :] = logits(t, tk)
    zb_ref[...] = logits(t - 1)
    step(za_ref[tk:, :], t, strict[:tq - tk], tk)
    za_ref[...] = logits(jnp.maximum(t - 2, 0))
    carry = step(zb_ref[...], t - 1, strict)

    def alive(state):
        p, top = state
        return (p <= i) & (top > SB_ZERO_LOG)

    def body(state):
        p, _ = state
        t = 2 * (i - p) + 1
        zb_ref[...] = logits(t - 1)
        top = jnp.max(step(za_ref[...], t, None))

        def second_tile():
            za_ref[...] = logits(jnp.maximum(t - 2, 0))
            return jnp.max(step(zb_ref[...], t - 1, None))

        return p + 1, lax.cond(top > SB_ZERO_LOG, second_tile, lambda: top)

    lax.while_loop(alive, body, (jnp.int32(1), jnp.max(carry)))
    o_ref[...] = acc_ref[...].astype(o_ref.dtype)


def _sb_attention(h, u, *, batch, seq, tq):
    nq = seq // tq
    nh = SB_HEADS
    kern = functools.partial(_sb_kernel, tq=tq)
    kcol = SB_WIDTH // HEAD_DIM
    return pl.pallas_call(
        kern,
        grid=(batch, nh, nq),
        in_specs=[pl.BlockSpec((tq, HEAD_DIM), lambda b, hh, i: (b * nq + i, hh)),
                  pl.BlockSpec((seq, HEAD_DIM), lambda b, hh, i: (b, kcol + hh)),
                  pl.BlockSpec((seq, HEAD_DIM), lambda b, hh, i: (b, 2 * kcol + hh)),
                  pl.BlockSpec((tq // 2, tq // 2), lambda b, hh, i: (0, 0))],
        out_specs=pl.BlockSpec((tq, HEAD_DIM), lambda b, hh, i: (b * nq + i, hh)),
        out_shape=jax.ShapeDtypeStruct((batch * seq, SB_WIDTH), BF16),
        scratch_shapes=[pltpu.VMEM((tq, HEAD_DIM), F32),
                        pltpu.VMEM((tq, LANES), F32),
                        pltpu.VMEM((tq, tq // 2), F32),
                        pltpu.VMEM((tq, tq // 2), F32)],
        compiler_params=_params(("parallel", "parallel", "arbitrary")),
        name="sb_attention",
    )(h, h, h, u)


def _softmax_step(s, vs, m_ref, l_ref, acc_ref):
    m_old = m_ref[...]
    m_new = jnp.maximum(m_old, jnp.max(s, axis=-1, keepdims=True))
    alpha = jnp.exp2(m_old - m_new)
    ps = [jnp.exp2(s[:, c * LANES:(c + 1) * LANES] - m_new) for c in range(s.shape[1] // LANES)]
    l_ref[...] = alpha * l_ref[...] + functools.reduce(lambda a, b: a + b, ps)
    pv = jnp.dot(jnp.concatenate(ps, axis=-1).astype(BF16), vs, preferred_element_type=F32)
    for c in range(pv.shape[1] // LANES):
        sl = slice(c * LANES, (c + 1) * LANES)
        acc_ref[:, sl] = alpha * acc_ref[:, sl] + pv[:, sl]
    m_ref[...] = m_new


def _causal_mask(tq, tk):
    row = lax.broadcasted_iota(jnp.int32, (tq, tk), 0)
    col = lax.broadcasted_iota(jnp.int32, (tq, tk), 1)
    return col <= row


def _diff_kernel(q_ref, k_ref, v_ref, lq1_ref, lk1_ref, lq2_ref, lk2_ref, g_ref, o_ref,
                 acc_ref, m_ref, l_ref, sa_ref, sb_ref, *, tq, lambda_init):
    tk = tq // 2
    i = pl.program_id(2)
    q = q_ref[...]
    m_ref[...] = jnp.full_like(m_ref, NEG_BIG)
    l_ref[...] = jnp.zeros_like(l_ref)
    acc_ref[...] = jnp.zeros_like(acc_ref)

    def scores(t, s_ref, r0=0):
        start = pl.multiple_of(t * tk, tk)
        ks = k_ref[pl.ds(start, tk), :]
        for mp in range(2):
            sl = slice(mp * HEAD_DIM, (mp + 1) * HEAD_DIM)
            s_ref[mp, r0:, :] = lax.dot_general(q[r0:, sl], ks[:, sl], _NT, preferred_element_type=F32)

    def step(s_ref, t, mask, r0=0):
        rows = pl.ds(r0, tq - r0)
        start = pl.multiple_of(t * tk, tk)
        vs = v_ref[pl.ds(start, tk), :]
        for mp in range(2):
            s = s_ref[mp, r0:, :]
            if mask is not None:
                s = jnp.where(mask, s, NEG_BIG)
            _softmax_step(s, vs, m_ref.at[mp, rows], l_ref.at[mp, rows], acc_ref.at[mp, rows])

    scores(0, sa_ref)

    def body(p, c):
        t = 2 * p
        scores(t + 1, sb_ref)
        step(sa_ref, t, None)
        scores(t + 2, sa_ref)
        step(sb_ref, t + 1, None)
        return c

    lax.fori_loop(0, i, body, 0)
    t = 2 * i
    scores(t + 1, sb_ref, tk)
    causal = _causal_mask(tq, tk)
    step(sa_ref, t, causal)
    step(sb_ref, t + 1, causal[:tq - tk], tk)

    lam = (jnp.exp(jnp.sum(lq1_ref[...] * lk1_ref[...], axis=-1, keepdims=True))
           - jnp.exp(jnp.sum(lq2_ref[...] * lk2_ref[...], axis=-1, keepdims=True))
           + lambda_init)
    l0 = jnp.sum(l_ref[0], axis=-1, keepdims=True)
    l1 = jnp.sum(l_ref[1], axis=-1, keepdims=True)
    o = acc_ref[0] / l0 - lam * (acc_ref[1] / l1)
    o = _rms(o, g_ref[...], LN_EPS) * (1.0 - lambda_init)
    o_ref[...] = o.astype(o_ref.dtype)


def _diff_attention(h, lq1, lk1, lq2, lk2, g, *, batch, seq, tq, lambda_init):
    nq = seq // tq
    kern = functools.partial(_diff_kernel, tq=tq, lambda_init=lambda_init)
    qcol = 3 * SB_WIDTH // DIFF_V_DIM
    kcol = qcol + DIFF_QK_WIDTH // DIFF_V_DIM
    vcol = kcol + DIFF_QK_WIDTH // DIFF_V_DIM
    vec = pl.BlockSpec((1, HEAD_DIM), lambda b, hh, i: (0, 0))
    return pl.pallas_call(
        kern,
        grid=(batch, DIFF_HEADS, nq),
        in_specs=[pl.BlockSpec((tq, DIFF_V_DIM), lambda b, hh, i: (b * nq + i, qcol + hh)),
                  pl.BlockSpec((seq, DIFF_V_DIM), lambda b, hh, i: (b, kcol + hh)),
                  pl.BlockSpec((seq, DIFF_V_DIM), lambda b, hh, i: (b, vcol + hh)),
                  vec, vec, vec, vec,
                  pl.BlockSpec((1, DIFF_V_DIM), lambda b, hh, i: (0, 0))],
        out_specs=pl.BlockSpec((tq, DIFF_V_DIM), lambda b, hh, i: (b * nq + i, hh)),
        out_shape=jax.ShapeDtypeStruct((batch * seq, DIFF_V_WIDTH), BF16),
        scratch_shapes=[pltpu.VMEM((2, tq, DIFF_V_DIM), F32),
                        pltpu.VMEM((2, tq, LANES), F32),
                        pltpu.VMEM((2, tq, LANES), F32),
                        pltpu.VMEM((2, tq, tq // 2), F32),
                        pltpu.VMEM((2, tq, tq // 2), F32)],
        compiler_params=_params(("parallel", "parallel", "arbitrary")),
        name="diff_attention",
    )(h, h, h, lq1, lk1, lq2, lk2, g)


def _mla_kernel(q_ref, kv_ref, kpe_ref, o_ref, acc_ref, m_ref, l_ref, sa_ref, sb_ref, *, tq, nh):
    tk = tq // 2
    wh = 2 * LANES
    i = pl.program_id(2)
    m_ref[...] = jnp.full_like(m_ref, NEG_BIG)
    l_ref[...] = jnp.zeros_like(l_ref)
    acc_ref[...] = jnp.zeros_like(acc_ref)

    def scores(t, s_ref, r0=0):
        start = pl.multiple_of(t * tk, tk)
        kpe = kpe_ref[pl.ds(start, tk), :]
        for h in range(nh):
            ks = jnp.concatenate([kv_ref[pl.ds(start, tk), h * wh:h * wh + LANES], kpe], axis=-1)
            s_ref[h, r0:, :] = lax.dot_general(q_ref[r0:, h * wh:(h + 1) * wh], ks, _NT,
                                               preferred_element_type=F32)

    def step(s_ref, t, mask, r0=0):
        rows = pl.ds(r0, tq - r0)
        start = pl.multiple_of(t * tk, tk)
        for h in range(nh):
            s = s_ref[h, r0:, :]
            if mask is not None:
                s = jnp.where(mask, s, NEG_BIG)
            _softmax_step(s, kv_ref[pl.ds(start, tk), h * wh + LANES:(h + 1) * wh],
                          m_ref.at[h, rows], l_ref.at[h, rows], acc_ref.at[h, rows])

    scores(0, sa_ref)

    def body(p, c):
        t = 2 * p
        scores(t + 1, sb_ref)
        step(sa_ref, t, None)
        scores(t + 2, sa_ref)
        step(sb_ref, t + 1, None)
        return c

    lax.fori_loop(0, i, body, 0)
    t = 2 * i
    scores(t + 1, sb_ref, tk)
    causal = _causal_mask(tq, tk)
    step(sa_ref, t, causal)
    step(sb_ref, t + 1, causal[:tq - tk], tk)
    for h in range(nh):
        l = jnp.sum(l_ref[h], axis=-1, keepdims=True)
        o_ref[:, h * MLA_V_DIM:(h + 1) * MLA_V_DIM] = (acc_ref[h] / l).astype(o_ref.dtype)


def _mla_attention(q, kv, kpe, *, batch, seq, tq, nh):
    nq = seq // tq
    tk = tq // 2
    kern = functools.partial(_mla_kernel, tq=tq, nh=nh)
    return pl.pallas_call(
        kern,
        grid=(batch, MLA_HEADS // nh, nq),
        in_specs=[pl.BlockSpec((tq, nh * 2 * LANES), lambda b, hp, i: (b * nq + i, hp)),
                  pl.BlockSpec((seq, nh * 2 * LANES), lambda b, hp, i: (b, hp)),
                  pl.BlockSpec((seq, LANES), lambda b, hp, i: (b, 0))],
        out_specs=pl.BlockSpec((tq, nh * MLA_V_DIM), lambda b, hp, i: (b * nq + i, hp)),
        out_shape=jax.ShapeDtypeStruct((batch * seq, MLA_HEADS * MLA_V_DIM), BF16),
        scratch_shapes=[pltpu.VMEM((nh, tq, MLA_V_DIM), F32),
                        pltpu.VMEM((nh, tq, LANES), F32),
                        pltpu.VMEM((nh, tq, LANES), F32),
                        pltpu.VMEM((nh, tq, tk), F32),
                        pltpu.VMEM((nh, tq, tk), F32)],
        compiler_params=_params(("parallel", "parallel", "arbitrary")),
        name="mla_attention",
    )(q, kv, kpe)


def _spread_rope_cols(w):
    half = MLA_ROPE_DIM // 2
    z = jnp.zeros(w.shape[:-1] + (LANES // 2 - half,), w.dtype)
    return jnp.concatenate([w[..., :half], z, w[..., half:], z], axis=-1)


def _rope_freq_row():
    f_full = ROPE_THETA ** (-jnp.arange(0, HEAD_DIM, 2, dtype=F32) / HEAD_DIM)
    f_mla = ROPE_THETA ** (-jnp.arange(0, MLA_ROPE_DIM, 2, dtype=F32) / MLA_ROPE_DIM)
    pad = jnp.zeros((LANES - f_full.shape[0] - f_mla.shape[0],), F32)
    return jnp.concatenate([f_full, f_mla, pad])[None]


def kernel(x, positions, sb_diff_w_in, sb_diff_w_out, diff_lambda_q1, diff_lambda_k1,
           diff_lambda_q2, diff_lambda_k2, diff_subln_g, mla_w_in, mla_q_norm_g, mla_kv_norm_g,
           mla_w_q_up, mla_w_kv_up, mla_w_out, ffn_w_gate, ffn_w_up, ffn_w_down, ln_g, ln_b):
    batch, seq, d = x.shape
    t = batch * seq
    tq = min(seq, 512)
    tq_soft = min(seq, 1024)
    sub = tq // 2
    bm = min(t, 1024)
    bm_ln = min(t, 512)

    cos_tab, sin_tab = _rope_tables(positions.reshape(t, 1), _rope_freq_row())
    u = (lax.broadcasted_iota(jnp.int32, (sub, sub), 0)
         > lax.broadcasted_iota(jnp.int32, (sub, sub), 1)).astype(BF16)

    xf = x.reshape(t, d)
    xb = xf.astype(BF16)

    def ffn(layer, xf, xb, want_bf16):
        h = _ffn_up(xb, ffn_w_gate, ffn_w_up, layer, bm=bm, bn=512)
        return _mm_res_ln(h, ffn_w_down[layer].astype(BF16), xf, ln_g[layer, 1][None], ln_b[layer, 1][None],
                          bm=256, row_parts=2, want_bf16=want_bf16, name=f"ffn_down{layer}")

    h0 = _in_proj0(xb, sb_diff_w_in[0], cos_tab, sin_tab, bm=bm, bn=1024)
    oa = _sb_attention(h0, u, batch=batch, seq=seq, tq=tq)
    lambda_init = 0.8 - 0.6 * math.exp(-0.3 * 0)
    od = _diff_attention(h0, diff_lambda_q1[0][None], diff_lambda_k1[0][None], diff_lambda_q2[0][None],
                         diff_lambda_k2[0][None], diff_subln_g[0][None],
                         batch=batch, seq=seq, tq=tq_soft, lambda_init=lambda_init)
    mix = jnp.concatenate([oa, od], axis=-1)
    xf, xb = _mm_res_ln(mix, sb_diff_w_out[0].astype(BF16), xf, ln_g[0, 0][None], ln_b[0, 0][None],
                        bm=bm_ln, row_parts=2, want_bf16=True, name="attn_out0")
    xf, xb = ffn(0, xf, xb, True)

    w_in1 = jnp.concatenate([mla_w_in[0][:, :MLA_Q_RANK + MLA_KV_RANK],
                             _spread_rope_cols(mla_w_in[0][:, MLA_Q_RANK + MLA_KV_RANK:])],
                            axis=-1).astype(BF16)
    wq = mla_w_q_up[0].reshape(MLA_Q_RANK, MLA_HEADS, MLA_QK_DIM)
    wq = jnp.concatenate([wq[..., :MLA_NOPE_DIM], _spread_rope_cols(wq[..., MLA_NOPE_DIM:])], axis=-1)
    wq = wq.reshape(MLA_Q_RANK, MLA_HEADS * 2 * LANES).astype(BF16)
    q1, kv1, kpe = _mla_proj(xb, w_in1, mla_q_norm_g[0][None], mla_kv_norm_g[0][None], cos_tab, sin_tab,
                             wq, mla_w_kv_up[0].astype(BF16), bm=bm_ln)
    o1 = _mla_attention(q1, kv1, kpe, batch=batch, seq=seq, tq=tq_soft, nh=2)
    xf, xb = _mm_res_ln(o1, mla_w_out[0].astype(BF16), xf, ln_g[1, 0][None], ln_b[1, 0][None],
                        bm=bm_ln, row_parts=2, want_bf16=True, name="attn_out1")
    xf, _ = ffn(1, xf, xb, False)
    return xf.reshape(batch, seq, d)
```
